```python
import jax, jax.numpy as jnp
from jax import lax
import numpy as np

D_MODEL = 2048
BATCH = 4
SEQ = 4096
DEPTH = 4

CHUNK = 64
QBLOCK = 128
N_A_LAYERS = DEPTH // 2
N_B_LAYERS = DEPTH - N_A_LAYERS
EXPAND = 2
W_A = EXPAND * D_MODEL
POOL_WINDOWS = (2, 4, 8, 16)
N_POOL_GROUPS = len(POOL_WINDOWS)
G_A = W_A // N_POOL_GROUPS
N_HEADS = 16
QK_NOPE = 128
QK_ROPE = 64
QK_HEAD = QK_NOPE + QK_ROPE
V_HEAD = 128
KV_LORA = 512
Q_LORA = 512
W_B = N_HEADS * V_HEAD
ROPE_THETA = 10000.0
EPS = 1e-6

kernel_name = "yoco_pool_mla_gated_hybrid"


def rmsnorm(x, g):
    x32 = x.astype(jnp.float32)
    r = x32 * lax.rsqrt(jnp.mean(x32 * x32, axis=-1, keepdims=True) + EPS)
    return (r * g.astype(jnp.float32)).astype(x.dtype)


def rope_tables(positions):
    inv = ROPE_THETA ** (-jnp.arange(0, QK_ROPE, 2, dtype=jnp.float32) / QK_ROPE)
    ang = positions.astype(jnp.float32)[..., None] * inv
    return jnp.cos(ang), jnp.sin(ang)


def apply_rope(t, cos, sin):
    c = cos[:, :, None, :].astype(t.dtype)
    s = sin[:, :, None, :].astype(t.dtype)
    t1, t2 = t[..., : QK_ROPE // 2], t[..., QK_ROPE // 2:]
    return jnp.concatenate([t1 * c - t2 * s, t2 * c + t1 * s], axis=-1)


def qk_norm_rope(t, g, cos, sin):
    t = rmsnorm(t, g)
    return jnp.concatenate([t[..., :QK_NOPE], apply_rope(t[..., QK_NOPE:], cos, sin)], axis=-1)


def causal_mean_pool(u, w):
    B, S, C = u.shape
    u32 = u.astype(jnp.float32)
    c0 = jnp.concatenate([jnp.zeros((B, 1, C), jnp.float32), jnp.cumsum(u32, axis=1)], axis=1)
    padded = jnp.pad(c0, ((0, 0), (w - 1, 0), (0, 0)))
    window_sum = c0[:, 1:] - padded[:, :S]
    count = jnp.minimum(jnp.arange(1, S + 1, dtype=jnp.float32), float(w))
    return (window_sum / count[None, :, None]).astype(u.dtype)


def pool_mixer_layer(x, norm_g, w_in, w_group, scale, w_out):
    B, S, _ = x.shape
    h = rmsnorm(x, norm_g)
    proj = h @ w_in
    u, gate = proj[..., :W_A], proj[..., W_A:]
    u = u.reshape(B, S, N_POOL_GROUPS, G_A)
    pooled = jnp.stack(
        [causal_mean_pool(u[:, :, gi], w) - u[:, :, gi] for gi, w in enumerate(POOL_WINDOWS)],
        axis=2)
    y = jnp.einsum('bsgc,gcd->bsgd', pooled, w_group).reshape(B, S, W_A) * scale
    y = y * jax.nn.silu(gate)
    return x + y @ w_out


def shared_kv(x, kv_norm_g, kv_w_a, kv_latent_g, kv_w_b, k_norm_g, cos, sin):
    B, S, _ = x.shape
    h = rmsnorm(x, kv_norm_g)
    ckv = h @ kv_w_a
    c, k_rope = ckv[..., :KV_LORA], ckv[..., KV_LORA:]
    c = rmsnorm(c, kv_latent_g)
    kv = (c @ kv_w_b).reshape(B, S, N_HEADS, QK_NOPE + V_HEAD)
    k_nope, v = kv[..., :QK_NOPE], kv[..., QK_NOPE:]
    k_rope = jnp.broadcast_to(k_rope[:, :, None, :], (B, S, N_HEADS, QK_ROPE))
    k = qk_norm_rope(jnp.concatenate([k_nope, k_rope], axis=-1), k_norm_g, cos, sin)
    return k, v


def chunk_causal_attention(q, k, v):
    S = q.shape[1]
    scale = QK_HEAD ** -0.5
    outs = []
    for i in range(S // QBLOCK):
        q0, k_end = i * QBLOCK, (i + 1) * QBLOCK
        s = jnp.einsum('bqhd,bkhd->bhqk', q[:, q0:k_end], k[:, :k_end],
                       preferred_element_type=jnp.float32) * scale
        q_chunk = (q0 + jnp.arange(QBLOCK)) // CHUNK
        k_chunk = jnp.arange(k_end) // CHUNK
        mask = k_chunk[None, :] <= q_chunk[:, None]
        s = jnp.where(mask[None, None], s, -1e30)
        p = jax.nn.softmax(s, axis=-1).astype(v.dtype)
        outs.append(jnp.einsum('bhqk,bkhd->bqhd', p, v[:, :k_end]))
    return jnp.concatenate(outs, axis=1)


def mla_layer(x, k, v, norm_g, w_in, q_latent_g, w_q_b, q_norm_g, w_out, cos, sin):
    B, S, _ = x.shape
    h = rmsnorm(x, norm_g)
    proj = h @ w_in
    q_lat, gate = proj[..., :Q_LORA], proj[..., Q_LORA:]
    q = (rmsnorm(q_lat, q_latent_g) @ w_q_b).reshape(B, S, N_HEADS, QK_HEAD)
    q = qk_norm_rope(q, q_norm_g, cos, sin)
    o = chunk_causal_attention(q, k, v).reshape(B, S, W_B)
    return x + (o * jax.nn.silu(gate)) @ w_out


def setup_inputs(seed: int = 0) -> dict:
    key = jax.random.key(seed)
    ks = jax.random.split(key, 24)
    f32 = jnp.float32

    def w(k, shape, fan_in):
        return jax.random.normal(k, shape, f32) * (fan_in ** -0.5)

    def gain(k, shape):
        return 1.0 + 0.02 * jax.random.normal(k, shape, f32)

    x = jax.random.normal(ks[0], (BATCH, SEQ, D_MODEL), f32)
    offsets = jax.random.randint(ks[1], (BATCH,), 0, 64, dtype=jnp.int32) * CHUNK
    positions = (offsets[:, None] + jnp.arange(SEQ, dtype=jnp.int32)[None, :]).astype(jnp.int32)
    return {
        "x": x,
        "positions": positions,
        "a_norm_g": gain(ks[2], (N_A_LAYERS, D_MODEL)),
        "a_w_in": w(ks[3], (N_A_LAYERS, D_MODEL, 2 * W_A), D_MODEL),
        "a_w_group": w(ks[4], (N_A_LAYERS, N_POOL_GROUPS, G_A, G_A), G_A),
        "a_scale": gain(ks[5], (N_A_LAYERS, W_A)),
        "a_w_out": w(ks[6], (N_A_LAYERS, W_A, D_MODEL), W_A),
        "kv_norm_g": gain(ks[7], (D_MODEL,)),
        "kv_w_a": w(ks[8], (D_MODEL, KV_LORA + QK_ROPE), D_MODEL),
        "kv_latent_g": gain(ks[9], (KV_LORA,)),
        "kv_w_b": w(ks[10], (KV_LORA, N_HEADS * (QK_NOPE + V_HEAD)), KV_LORA),
        "k_norm_g": gain(ks[11], (QK_HEAD,)),
        "b_norm_g": gain(ks[12], (N_B_LAYERS, D_MODEL)),
        "b_w_in": w(ks[13], (N_B_LAYERS, D_MODEL, Q_LORA + W_B), D_MODEL),
        "b_q_latent_g": gain(ks[14], (N_B_LAYERS, Q_LORA)),
        "b_w_q_b": w(ks[15], (N_B_LAYERS, Q_LORA, N_HEADS * QK_HEAD), Q_LORA),
        "b_q_norm_g": gain(ks[16], (N_B_LAYERS, QK_HEAD)),
        "b_w_out": w(ks[17], (N_B_LAYERS, W_B, D_MODEL), W_B),
    }


def reference(x, positions, a_norm_g, a_w_in, a_w_group, a_scale, a_w_out,
              kv_norm_g, kv_w_a, kv_latent_g, kv_w_b, k_norm_g,
              b_norm_g, b_w_in, b_q_latent_g, b_w_q_b, b_q_norm_g, b_w_out):
    cos, sin = rope_tables(positions)
    k = v = None
    for layer in range(DEPTH):
        if layer < N_A_LAYERS:
            x = pool_mixer_layer(x, a_norm_g[layer], a_w_in[layer], a_w_group[layer],
                                 a_scale[layer], a_w_out[layer])
        else:
            if layer == N_A_LAYERS:
                k, v = shared_kv(x, kv_norm_g, kv_w_a, kv_latent_g, kv_w_b, k_norm_g, cos, sin)
            j = layer - N_A_LAYERS
            x = mla_layer(x, k, v, b_norm_g[j], b_w_in[j], b_q_latent_g[j], b_w_q_b[j],
                          b_q_norm_g[j], b_w_out[j], cos, sin)
    return x
```

```python
import functools
import math

import jax
import jax.numpy as jnp
from jax import lax
from jax.experimental import pallas as pl
from jax.experimental.pallas import tpu as pltpu

CHUNK = 64
POOL_WINDOWS = (2, 4, 8, 16)
N_HEADS = 16
QK_NOPE = 128
QK_ROPE = 64
QK_HEAD = QK_NOPE + QK_ROPE
V_HEAD = 128
KV_LORA = 512
Q_LORA = 512
ROPE_THETA = 10000.0
EPS = 1e-6

LANES = 128
POOL_HALO = 32
VMEM_LIMIT_BYTES = 56 * 1024 * 1024

F32 = jnp.float32
BF16 = jnp.bfloat16
NT_DIMS = (((1,), (1,)), ((), ()))


def _cparams(n_axes):
    return pltpu.CompilerParams(
        dimension_semantics=("arbitrary",) * n_axes,
        vmem_limit_bytes=VMEM_LIMIT_BYTES)


def _resident(shape, index_map):
    return pl.BlockSpec(shape, index_map, pipeline_mode=pl.Buffered(1))


def _silu(v):
    return v * jax.nn.sigmoid(v)


def _norm_kernel(x_ref, g_ref, h_ref):
    x = x_ref[0]
    ms = jnp.mean(x * x, axis=-1, keepdims=True)
    h_ref[0] = (x * lax.rsqrt(ms + EPS) * g_ref[...]).astype(BF16)


def _norm(x, g, tm):
    b, s, d = x.shape
    return pl.pallas_call(
        _norm_kernel,
        grid=(b, s // tm),
        in_specs=[pl.BlockSpec((1, tm, d), lambda i, r: (i, r, 0)),
                  _resident((1, d), lambda i, r: (0, 0))],
        out_specs=pl.BlockSpec((1, tm, d), lambda i, r: (i, r, 0)),
        out_shape=jax.ShapeDtypeStruct((b, s, d), BF16),
        compiler_params=_cparams(2),
        name="norm",
    )(x, g.reshape(1, d))


def _pool_kernel(h_ref, wu_ref, wg_ref, wgrp_ref, sc_ref, y_ref,
                 a_ref, b_ref, c_ref, p_ref, *, tm, levels):
    r = pl.program_id(1)
    ga = a_ref.shape[1]

    @pl.when(r == 0)
    def _():
        a_ref[0:POOL_HALO, :] = jnp.zeros((POOL_HALO, ga), F32)

    h = h_ref[0]
    a_ref[POOL_HALO:POOL_HALO + tm, :] = jnp.dot(h, wu_ref[...], preferred_element_type=F32)

    src = a_ref
    for j in range(levels):
        shift = 1 << j
        lo = 8 * (j + 1)
        n = POOL_HALO + tm - lo
        dst = (b_ref, c_ref)[j % 2]
        dst[lo:lo + n, :] = src[lo:lo + n, :] + src[lo - shift:lo - shift + n, :]
        src = dst

    window = 1 << levels
    t = r * tm + lax.broadcasted_iota(jnp.int32, (tm, LANES), 0)
    inv_count = 1.0 / jnp.minimum(t + 1, window).astype(F32)
    for c in range(ga // LANES):
        cols = slice(c * LANES, (c + 1) * LANES)
        pooled = src[POOL_HALO:POOL_HALO + tm, cols] * inv_count - a_ref[POOL_HALO:POOL_HALO + tm, cols]
        p_ref[:, cols] = pooled.astype(BF16)

    a_ref[0:POOL_HALO, :] = a_ref[tm:tm + POOL_HALO, :]

    z = jnp.dot(p_ref[...], wgrp_ref[...], preferred_element_type=F32)
    gate = jnp.dot(h, wg_ref[...], preferred_element_type=F32)
    y_ref[0] = (z * sc_ref[...] * _silu(gate)).astype(BF16)


def _pool_group(h, w_in, w_group, scale, g, tm):
    b, s, d = h.shape
    n_groups, ga, _ = w_group.shape
    levels = int(math.log2(POOL_WINDOWS[g]))
    assert 1 << levels == POOL_WINDOWS[g] and 8 * levels <= POOL_HALO <= tm
    return pl.pallas_call(
        functools.partial(_pool_kernel, tm=tm, levels=levels),
        grid=(b, s // tm),
        in_specs=[pl.BlockSpec((1, tm, d), lambda i, r: (i, r, 0)),
                  _resident((d, ga), lambda i, r: (0, g)),
                  _resident((d, ga), lambda i, r: (0, n_groups + g)),
                  _resident((None, ga, ga), lambda i, r: (g, 0, 0)),
                  _resident((1, ga), lambda i, r: (0, g))],
        out_specs=pl.BlockSpec((1, tm, ga), lambda i, r: (i, r, 0)),
        out_shape=jax.ShapeDtypeStruct((b, s, ga), BF16),
        scratch_shapes=[pltpu.VMEM((POOL_HALO + tm, ga), F32),
                        pltpu.VMEM((POOL_HALO + tm, ga), F32),
                        pltpu.VMEM((POOL_HALO + tm, ga), F32),
                        pltpu.VMEM((tm, ga), BF16)],
        compiler_params=_cparams(2),
        name=f"pool_g{g}",
    )(h, w_in, w_in, w_group, scale)


def _out_kernel(*refs, n_parts, n_norm):
    y_refs = refs[:n_parts]
    w_ref, x_ref = refs[n_parts], refs[n_parts + 1]
    g_ref = refs[n_parts + 2] if n_norm else None
    outs = refs[n_parts + 2 + (1 if n_norm else 0):]
    xo_ref, h_refs = outs[0], outs[1:]

    acc = x_ref[0]
    for p in range(n_parts):
        acc = acc + jnp.dot(y_refs[p][0], w_ref[p], preferred_element_type=F32)
    xo_ref[0] = acc
    if n_norm:
        ms = jnp.mean(acc * acc, axis=-1, keepdims=True)
        rn = acc * lax.rsqrt(ms + EPS)
        for j in range(n_norm):
            h_refs[j][0] = (rn * g_ref[j:j + 1, :]).astype(BF16)


def _out_proj(parts, w, x, gains, tm):
    b, s, d = x.shape
    n_parts, kp, _ = w.shape
    n_norm = 0 if gains is None else gains.shape[0]
    row = lambda i, r: (i, r, 0)
    in_specs = [pl.BlockSpec((1, tm, kp), row) for _ in range(n_parts)]
    in_specs += [_resident((n_parts, kp, d), lambda i, r: (0, 0, 0)),
                 pl.BlockSpec((1, tm, d), row)]
    args = list(parts) + [w, x]
    if n_norm:
        in_specs.append(_resident((n_norm, d), lambda i, r: (0, 0)))
        args.append(gains)
    out_specs = [pl.BlockSpec((1, tm, d), row) for _ in range(1 + n_norm)]
    out_shape = [jax.ShapeDtypeStruct((b, s, d), F32)]
    out_shape += [jax.ShapeDtypeStruct((b, s, d), BF16) for _ in range(n_norm)]
    return pl.pallas_call(
        functools.partial(_out_kernel, n_parts=n_parts, n_norm=n_norm),
        grid=(b, s // tm),
        in_specs=in_specs,
        out_specs=out_specs,
        out_shape=out_shape,
        compiler_params=_cparams(2),
        name=f"out_proj_p{n_parts}_n{n_norm}",
    )(*args)


def _kv_kernel(h_ref, wa_ref, glat_ref, wk_ref, wvt_ref, gkn_ref, gkr_ref, cos_ref, sin_ref,
               k_ref, vt_ref, *, tm, tkv):
    h = h_ref[0]
    ckv = jnp.dot(h, wa_ref[...], preferred_element_type=F32)
    c = ckv[:, :KV_LORA]
    kr = ckv[:, KV_LORA:]
    cn = (c * lax.rsqrt(jnp.mean(c * c, axis=-1, keepdims=True) + EPS) * glat_ref[...]).astype(BF16)
    kn = jnp.dot(cn, wk_ref[...], preferred_element_type=F32)
    vt = lax.dot_general(wvt_ref[...], cn, NT_DIMS, preferred_element_type=F32)

    ss_rope = jnp.sum(kr * kr, axis=-1, keepdims=True)
    krg = kr * gkr_ref[...]
    half = QK_ROPE // 2
    swapped = pltpu.roll(krg, half, 1) + pltpu.roll(krg, LANES - half, 1)
    roped = krg * cos_ref[0] + swapped * sin_ref[0]

    for hd in range(N_HEADS):
        knh = kn[:, hd * QK_NOPE:(hd + 1) * QK_NOPE]
        ss = jnp.sum(knh * knh, axis=-1, keepdims=True) + ss_rope
        rinv = lax.rsqrt(ss * (1.0 / QK_HEAD) + EPS)
        k_ref[0, hd, :, 0:QK_NOPE] = (knh * rinv * gkn_ref[...]).astype(BF16)
        k_ref[0, hd, :, QK_NOPE:QK_HEAD] = (roped[:, :QK_ROPE] * rinv).astype(BF16)
        for j in range(tm // tkv):
            vt_ref[0, hd, j] = vt[hd * V_HEAD:(hd + 1) * V_HEAD, j * tkv:(j + 1) * tkv].astype(BF16)


def _shared_kv(h, wa, glat, wk, wvt, gkn, gkr, cos_k, sin_k, tm, tkv):
    b, s, d = h.shape
    row = lambda i, r: (i, r, 0)
    const2 = lambda i, r: (0, 0)
    return pl.pallas_call(
        functools.partial(_kv_kernel, tm=tm, tkv=tkv),
        grid=(b, s // tm),
        in_specs=[pl.BlockSpec((1, tm, d), row),
                  _resident(wa.shape, const2),
                  _resident(glat.shape, const2),
                  _resident(wk.shape, const2),
                  _resident(wvt.shape, const2),
                  _resident(gkn.shape, const2),
                  _resident(gkr.shape, const2),
                  pl.BlockSpec((1, tm, LANES), row),
                  pl.BlockSpec((1, tm, LANES), row)],
        out_specs=[pl.BlockSpec((1, N_HEADS, tm, QK_HEAD), lambda i, r: (i, 0, r, 0)),
                   pl.BlockSpec((1, N_HEADS, tm // tkv, V_HEAD, tkv), lambda i, r: (i, 0, r, 0, 0))],
        out_shape=[jax.ShapeDtypeStruct((b, N_HEADS, s, QK_HEAD), BF16),
                   jax.ShapeDtypeStruct((b, N_HEADS, s // tkv, V_HEAD, tkv), BF16)],
        compiler_params=_cparams(2),
        name="shared_kv",
    )(h, wa, glat, wk, wvt, gkn, gkr, cos_k, sin_k)


def _q_kernel(h_ref, wlat_ref, glat_ref, wgate_ref, wqt_ref, gq_ref, cos_ref, sin_ref,
              qt_ref, sg_ref, *, tm, q_scale):
    h = h_ref[0]
    ql = jnp.dot(h, wlat_ref[...], preferred_element_type=F32)
    qn = (ql * lax.rsqrt(jnp.mean(ql * ql, axis=-1, keepdims=True) + EPS) * glat_ref[...]).astype(BF16)
    gate = jnp.dot(h, wgate_ref[...], preferred_element_type=F32)
    sg_ref[0] = _silu(gate).astype(BF16)

    qt = lax.dot_general(wqt_ref[...], qn, NT_DIMS, preferred_element_type=F32)
    gq = jnp.broadcast_to(gq_ref[...], (QK_HEAD, tm))
    cos = cos_ref[0]
    sin = sin_ref[0]
    half = QK_ROPE // 2
    for hd in range(N_HEADS):
        q = qt[hd * QK_HEAD:(hd + 1) * QK_HEAD, :]
        ss = jnp.sum(q * q, axis=0, keepdims=True)
        rinv = lax.rsqrt(ss * (1.0 / QK_HEAD) + EPS) * q_scale
        qg = q * gq
        t1 = qg[QK_NOPE:QK_NOPE + half]
        t2 = qg[QK_NOPE + half:]
        qt_ref[0, hd, 0:QK_NOPE, :] = (qg[:QK_NOPE] * rinv).astype(BF16)
        qt_ref[0, hd, QK_NOPE:QK_NOPE + half, :] = ((t1 * cos - t2 * sin) * rinv).astype(BF16)
        qt_ref[0, hd, QK_NOPE + half:QK_HEAD, :] = ((t2 * cos + t1 * sin) * rinv).astype(BF16)


def _q_proj(h, wlat, glat, wgate, wqt, gq, cos_t, sin_t, tm, q_scale):
    b, s, d = h.shape
    const2 = lambda i, r: (0, 0)
    wb = wgate.shape[1]
    return pl.pallas_call(
        functools.partial(_q_kernel, tm=tm, q_scale=q_scale),
        grid=(b, s // tm),
        in_specs=[pl.BlockSpec((1, tm, d), lambda i, r: (i, r, 0)),
                  _resident(wlat.shape, const2),
                  _resident(glat.shape, const2),
                  _resident(wgate.shape, const2),
                  _resident(wqt.shape, const2),
                  _resident(gq.shape, const2),
                  pl.BlockSpec((1, QK_ROPE // 2, tm), lambda i, r: (i, 0, r)),
                  pl.BlockSpec((1, QK_ROPE // 2, tm), lambda i, r: (i, 0, r))],
        out_specs=[pl.BlockSpec((1, N_HEADS, QK_HEAD, tm), lambda i, r: (i, 0, 0, r)),
                   pl.BlockSpec((1, tm, wb), lambda i, r: (i, r, 0))],
        out_shape=[jax.ShapeDtypeStruct((b, N_HEADS, QK_HEAD, s), BF16),
                   jax.ShapeDtypeStruct((b, s, wb), BF16)],
        compiler_params=_cparams(2),
        name="q_proj",
    )(h, wlat, glat, wgate, wqt, gq, cos_t, sin_t)


def _attn_kernel(qt_ref, k_ref, vt_ref, sg_ref, o_ref, m_ref, l_ref, acc_ref, *, tq, tk):
    qi = pl.program_id(2)
    qt = qt_ref[0, 0]
    m_ref[...] = jnp.full(m_ref.shape, -1e30, F32)
    l_ref[...] = jnp.zeros(l_ref.shape, F32)
    acc_ref[...] = jnp.zeros(acc_ref.shape, F32)

    def step(kt, masked):
        start = pl.multiple_of(kt * tk, tk)
        k = k_ref[0, 0, pl.ds(start, tk), :]
        st = jnp.dot(k, qt, preferred_element_type=F32)
        if masked:
            k_chunk = (kt * tk + lax.broadcasted_iota(jnp.int32, (tk, tq), 0)) // CHUNK
            q_chunk = (qi * tq + lax.broadcasted_iota(jnp.int32, (tk, tq), 1)) // CHUNK
            st = jnp.where(k_chunk <= q_chunk, st, -1e30)
        m_old = m_ref[...]
        m_new = jnp.maximum(m_old, jnp.max(st, axis=0, keepdims=True))
        alpha = jnp.exp2(m_old - m_new)
        p = jnp.exp2(st - m_new)
        l_ref[...] = alpha * l_ref[...] + jnp.sum(p, axis=0, keepdims=True)
        m_ref[...] = m_new
        pv = jnp.dot(vt_ref[0, 0, kt], p.astype(BF16), preferred_element_type=F32)
        acc_ref[...] = alpha * acc_ref[...] + pv

    n_diag = tq // tk
    n_full = qi * n_diag

    def body(kt, carry):
        step(kt, masked=False)
        return carry

    lax.fori_loop(0, n_full, body, 0)
    for j in range(n_diag):
        step(n_full + j, masked=True)

    o = (acc_ref[...] * (1.0 / l_ref[...])).T
    o_ref[0] = (o * sg_ref[0].astype(F32)).astype(BF16)


def _attention(qt, k, vt, sg, tq, tk):
    b, n_heads, _, s = qt.shape
    wb = sg.shape[2]
    n_kt = s // tk
    return pl.pallas_call(
        functools.partial(_attn_kernel, tq=tq, tk=tk),
        grid=(b, n_heads, s // tq),
        in_specs=[pl.BlockSpec((1, 1, QK_HEAD, tq), lambda i, hd, q: (i, hd, 0, q)),
                  pl.BlockSpec((1, 1, s, QK_HEAD), lambda i, hd, q: (i, hd, 0, 0)),
                  pl.BlockSpec((1, 1, n_kt, V_HEAD, tk), lambda i, hd, q: (i, hd, 0, 0, 0)),
                  pl.BlockSpec((1, tq, V_HEAD), lambda i, hd, q: (i, q, hd))],
        out_specs=pl.BlockSpec((1, tq, V_HEAD), lambda i, hd, q: (i, q, hd)),
        out_shape=jax.ShapeDtypeStruct((b, s, wb), BF16),
        scratch_shapes=[pltpu.VMEM((1, tq), F32),
                        pltpu.VMEM((1, tq), F32),
                        pltpu.VMEM((V_HEAD, tq), F32)],
        compiler_params=_cparams(3),
        name="attn",
    )(qt, k, vt, sg)


def _rope_tables(positions):
    inv = ROPE_THETA ** (-jnp.arange(0, QK_ROPE, 2, dtype=F32) / QK_ROPE)
    ang = positions.astype(F32)[..., None] * inv
    return jnp.cos(ang), jnp.sin(ang)


def kernel(x, positions, a_norm_g, a_w_in, a_w_group, a_scale, a_w_out, kv_norm_g, kv_w_a, kv_latent_g, kv_w_b, k_norm_g, b_norm_g, b_w_in, b_q_latent_g, b_w_q_b, b_q_norm_g, b_w_out):
    b, s, d = x.shape
    n_a = a_w_in.shape[0]
    n_b = b_w_in.shape[0]
    n_groups, ga = a_w_group.shape[1], a_w_group.shape[2]
    tm = min(512, s)
    tq = min(512, s)
    tk = min(256, s)

    cos, sin = _rope_tables(positions)
    zeros = jnp.zeros((b, s, LANES - QK_ROPE), F32)
    cos_k = jnp.concatenate([cos, cos, zeros], axis=-1)
    sin_k = jnp.concatenate([-sin, sin, zeros], axis=-1)
    cos_t = jnp.swapaxes(cos, 1, 2)
    sin_t = jnp.swapaxes(sin, 1, 2)

    h = _norm(x, a_norm_g[0], tm)
    for layer in range(n_a):
        w_in = a_w_in[layer].astype(BF16)
        w_group = a_w_group[layer].astype(BF16)
        scale = a_scale[layer].reshape(1, n_groups * ga)
        ys = [_pool_group(h, w_in, w_group, scale, g, tm) for g in range(n_groups)]
        w_out = a_w_out[layer].astype(BF16).reshape(n_groups, ga, d)
        if layer + 1 < n_a:
            x, h = _out_proj(ys, w_out, x, a_norm_g[layer + 1:layer + 2], tm)
        else:
            x, h, h_kv = _out_proj(ys, w_out, x, jnp.stack([b_norm_g[0], kv_norm_g]), tm)

    wa = jnp.pad(kv_w_a, ((0, 0), (0, LANES - QK_ROPE))).astype(BF16)
    kv_w = kv_w_b.reshape(KV_LORA, N_HEADS, QK_NOPE + V_HEAD)
    wk = kv_w[:, :, :QK_NOPE].reshape(KV_LORA, N_HEADS * QK_NOPE).astype(BF16)
    wvt = kv_w[:, :, QK_NOPE:].reshape(KV_LORA, N_HEADS * V_HEAD).T.astype(BF16)
    gkn = k_norm_g[:QK_NOPE].reshape(1, QK_NOPE)
    gkr = jnp.pad(k_norm_g[QK_NOPE:], (0, LANES - QK_ROPE)).reshape(1, LANES)
    k, vt = _shared_kv(h_kv, wa, kv_latent_g.reshape(1, KV_LORA), wk, wvt, gkn, gkr,
                       cos_k, sin_k, tm, tk)

    q_scale = (QK_HEAD ** -0.5) * math.log2(math.e)
    for j in range(n_b):
        w_in = b_w_in[j].astype(BF16)
        qt, sg = _q_proj(h, w_in[:, :Q_LORA], b_q_latent_g[j].reshape(1, Q_LORA), w_in[:, Q_LORA:],
                         b_w_q_b[j].T.astype(BF16), b_q_norm_g[j].reshape(QK_HEAD, 1),
                         cos_t, sin_t, tm, q_scale)
        og = _attention(qt, k, vt, sg, tq, tk)
        w_out = b_w_out[j].astype(BF16)[None]
        if j + 1 < n_b:
            x, h = _out_proj([og], w_out, x, b_norm_g[j + 1:j + 2], tm)
        else:
            (x,) = _out_proj([og], w_out, x, None, tm)
    return x
```

```python
import functools
import math

import jax
import jax.numpy as jnp
from jax import lax
from jax.experimental import pallas as pl
from jax.experimental.pallas import tpu as pltpu

CHUNK = 64
POOL_WINDOWS = (2, 4, 8, 16)
N_HEADS = 16
QK_NOPE = 128
QK_ROPE = 64
QK_HEAD = QK_NOPE + QK_ROPE
V_HEAD = 128
KV_LORA = 512
Q_LORA = 512
ROPE_THETA = 10000.0
EPS = 1e-6

LANES = 128
POOL_HALO = 32
ATT_SUB = 256
VMEM_LIMIT_BYTES = 56 * 1024 * 1024

F32 = jnp.float32
BF16 = jnp.bfloat16
NT_DIMS = (((1,), (1,)), ((), ()))


def _cparams(n_axes):
    return pltpu.CompilerParams(
        dimension_semantics=("arbitrary",) * n_axes,
        vmem_limit_bytes=VMEM_LIMIT_BYTES)


def _resident(shape, index_map):
    return pl.BlockSpec(shape, index_map, pipeline_mode=pl.Buffered(1))


def _silu(v):
    return v * jax.nn.sigmoid(v)


def _norm_kernel(x_ref, g_ref, h_ref):
    x = x_ref[0]
    ms = jnp.mean(x * x, axis=-1, keepdims=True)
    h_ref[0] = (x * lax.rsqrt(ms + EPS) * g_ref[...]).astype(BF16)


def _norm(x, g, tm):
    b, s, d = x.shape
    return pl.pallas_call(
        _norm_kernel,
        grid=(b, s // tm),
        in_specs=[pl.BlockSpec((1, tm, d), lambda i, r: (i, r, 0)),
                  _resident((1, d), lambda i, r: (0, 0))],
        out_specs=pl.BlockSpec((1, tm, d), lambda i, r: (i, r, 0)),
        out_shape=jax.ShapeDtypeStruct((b, s, d), BF16),
        compiler_params=_cparams(2),
        name="norm",
    )(x, g.reshape(1, d))


def _pool_kernel(h_ref, wu_ref, wg_ref, wgrp_ref, sc_ref, y_ref,
                 a_ref, b_ref, c_ref, p_ref, *, tm, levels):
    r = pl.program_id(1)
    ga = a_ref.shape[1]

    @pl.when(r == 0)
    def _():
        a_ref[0:POOL_HALO, :] = jnp.zeros((POOL_HALO, ga), F32)

    h = h_ref[0]
    a_ref[POOL_HALO:POOL_HALO + tm, :] = jnp.dot(h, wu_ref[...], preferred_element_type=F32)

    src = a_ref
    for j in range(levels):
        shift = 1 << j
        lo = 8 * (j + 1)
        n = POOL_HALO + tm - lo
        dst = (b_ref, c_ref)[j % 2]
        dst[lo:lo + n, :] = src[lo:lo + n, :] + src[lo - shift:lo - shift + n, :]
        src = dst

    window = 1 << levels
    t = r * tm + lax.broadcasted_iota(jnp.int32, (tm, LANES), 0)
    inv_count = 1.0 / jnp.minimum(t + 1, window).astype(F32)
    for c in range(ga // LANES):
        cols = slice(c * LANES, (c + 1) * LANES)
        pooled = src[POOL_HALO:POOL_HALO + tm, cols] * inv_count - a_ref[POOL_HALO:POOL_HALO + tm, cols]
        p_ref[:, cols] = pooled.astype(BF16)

    a_ref[0:POOL_HALO, :] = a_ref[tm:tm + POOL_HALO, :]

    z = jnp.dot(p_ref[...], wgrp_ref[...], preferred_element_type=F32)
    gate = jnp.dot(h, wg_ref[...], preferred_element_type=F32)
    y_ref[0] = (z * sc_ref[...] * _silu(gate)).astype(BF16)


def _pool_group(h, w_in, w_group, scale, g, tm):
    b, s, d = h.shape
    n_groups, ga, _ = w_group.shape
    levels = int(math.log2(POOL_WINDOWS[g]))
    assert 1 << levels == POOL_WINDOWS[g] and 8 * levels <= POOL_HALO <= tm
    return pl.pallas_call(
        functools.partial(_pool_kernel, tm=tm, levels=levels),
        grid=(b, s // tm),
        in_specs=[pl.BlockSpec((1, tm, d), lambda i, r: (i, r, 0)),
                  _resident((d, ga), lambda i, r: (0, g)),
                  _resident((d, ga), lambda i, r: (0, n_groups + g)),
                  _resident((None, ga, ga), lambda i, r: (g, 0, 0)),
                  _resident((1, ga), lambda i, r: (0, g))],
        out_specs=pl.BlockSpec((1, tm, ga), lambda i, r: (i, r, 0)),
        out_shape=jax.ShapeDtypeStruct((b, s, ga), BF16),
        scratch_shapes=[pltpu.VMEM((POOL_HALO + tm, ga), F32),
                        pltpu.VMEM((POOL_HALO + tm, ga), F32),
                        pltpu.VMEM((POOL_HALO + tm, ga), F32),
                        pltpu.VMEM((tm, ga), BF16)],
        compiler_params=_cparams(2),
        name=f"pool_g{g}",
    )(h, w_in, w_in, w_group, scale)


def _out_kernel(*refs, n_parts, n_norm):
    y_refs = refs[:n_parts]
    w_ref, x_ref = refs[n_parts], refs[n_parts + 1]
    g_ref = refs[n_parts + 2] if n_norm else None
    outs = refs[n_parts + 2 + (1 if n_norm else 0):]
    xo_ref, h_refs = outs[0], outs[1:]

    acc = x_ref[0]
    for p in range(n_parts):
        acc = acc + jnp.dot(y_refs[p][0], w_ref[p], preferred_element_type=F32)
    xo_ref[0] = acc
    if n_norm:
        ms = jnp.mean(acc * acc, axis=-1, keepdims=True)
        rn = acc * lax.rsqrt(ms + EPS)
        for j in range(n_norm):
            h_refs[j][0] = (rn * g_ref[j:j + 1, :]).astype(BF16)


def _out_proj(parts, w, x, gains, tm):
    b, s, d = x.shape
    n_parts, kp, _ = w.shape
    n_norm = 0 if gains is None else gains.shape[0]
    row = lambda i, r: (i, r, 0)
    in_specs = [pl.BlockSpec((1, tm, kp), row) for _ in range(n_parts)]
    in_specs += [_resident((n_parts, kp, d), lambda i, r: (0, 0, 0)),
                 pl.BlockSpec((1, tm, d), row)]
    args = list(parts) + [w, x]
    if n_norm:
        in_specs.append(_resident((n_norm, d), lambda i, r: (0, 0)))
        args.append(gains)
    out_specs = [pl.BlockSpec((1, tm, d), row) for _ in range(1 + n_norm)]
    out_shape = [jax.ShapeDtypeStruct((b, s, d), F32)]
    out_shape += [jax.ShapeDtypeStruct((b, s, d), BF16) for _ in range(n_norm)]
    return pl.pallas_call(
        functools.partial(_out_kernel, n_parts=n_parts, n_norm=n_norm),
        grid=(b, s // tm),
        in_specs=in_specs,
        out_specs=out_specs,
        out_shape=out_shape,
        compiler_params=_cparams(2),
        name=f"out_proj_p{n_parts}_n{n_norm}",
    )(*args)


def _kv_kernel(h_ref, wa_ref, glat_ref, wk_ref, wvt_ref, gkn_ref, gkr_ref, cos_ref, sin_ref,
               k_ref, vt_ref, *, tm, tkv):
    h = h_ref[0]
    ckv = jnp.dot(h, wa_ref[...], preferred_element_type=F32)
    c = ckv[:, :KV_LORA]
    kr = ckv[:, KV_LORA:]
    cn = (c * lax.rsqrt(jnp.mean(c * c, axis=-1, keepdims=True) + EPS) * glat_ref[...]).astype(BF16)
    kn = jnp.dot(cn, wk_ref[...], preferred_element_type=F32)
    vt = lax.dot_general(wvt_ref[...], cn, NT_DIMS, preferred_element_type=F32)

    ss_rope = jnp.sum(kr * kr, axis=-1, keepdims=True)
    krg = kr * gkr_ref[...]
    half = QK_ROPE // 2
    swapped = pltpu.roll(krg, half, 1) + pltpu.roll(krg, LANES - half, 1)
    roped = krg * cos_ref[0] + swapped * sin_ref[0]

    for hd in range(N_HEADS):
        knh = kn[:, hd * QK_NOPE:(hd + 1) * QK_NOPE]
        ss = jnp.sum(knh * knh, axis=-1, keepdims=True) + ss_rope
        rinv = lax.rsqrt(ss * (1.0 / QK_HEAD) + EPS)
        k_ref[0, hd, :, 0:QK_NOPE] = (knh * rinv * gkn_ref[...]).astype(BF16)
        k_ref[0, hd, :, QK_NOPE:QK_HEAD] = (roped[:, :QK_ROPE] * rinv).astype(BF16)
        for j in range(tm // tkv):
            vt_ref[0, hd, j] = vt[hd * V_HEAD:(hd + 1) * V_HEAD, j * tkv:(j + 1) * tkv].astype(BF16)


def _shared_kv(h, wa, glat, wk, wvt, gkn, gkr, cos_k, sin_k, tm, tkv):
    b, s, d = h.shape
    row = lambda i, r: (i, r, 0)
    const2 = lambda i, r: (0, 0)
    return pl.pallas_call(
        functools.partial(_kv_kernel, tm=tm, tkv=tkv),
        grid=(b, s // tm),
        in_specs=[pl.BlockSpec((1, tm, d), row),
                  _resident(wa.shape, const2),
                  _resident(glat.shape, const2),
                  _resident(wk.shape, const2),
                  _resident(wvt.shape, const2),
                  _resident(gkn.shape, const2),
                  _resident(gkr.shape, const2),
                  pl.BlockSpec((1, tm, LANES), row),
                  pl.BlockSpec((1, tm, LANES), row)],
        out_specs=[pl.BlockSpec((1, N_HEADS, tm, QK_HEAD), lambda i, r: (i, 0, r, 0)),
                   pl.BlockSpec((1, N_HEADS, tm // tkv, V_HEAD, tkv), lambda i, r: (i, 0, r, 0, 0))],
        out_shape=[jax.ShapeDtypeStruct((b, N_HEADS, s, QK_HEAD), BF16),
                   jax.ShapeDtypeStruct((b, N_HEADS, s // tkv, V_HEAD, tkv), BF16)],
        compiler_params=_cparams(2),
        name="shared_kv",
    )(h, wa, glat, wk, wvt, gkn, gkr, cos_k, sin_k)


def _q_kernel(h_ref, wlat_ref, glat_ref, wgate_ref, wqt_ref, gq_ref, cos_ref, sin_ref,
              qt_ref, sg_ref, *, tm, q_scale):
    h = h_ref[0]
    ql = jnp.dot(h, wlat_ref[...], preferred_element_type=F32)
    qn = (ql * lax.rsqrt(jnp.mean(ql * ql, axis=-1, keepdims=True) + EPS) * glat_ref[...]).astype(BF16)
    gate = jnp.dot(h, wgate_ref[...], preferred_element_type=F32)
    sg_ref[0] = _silu(gate).astype(BF16)

    qt = lax.dot_general(wqt_ref[...], qn, NT_DIMS, preferred_element_type=F32)
    gq = jnp.broadcast_to(gq_ref[...], (QK_HEAD, tm))
    cos = cos_ref[0]
    sin = sin_ref[0]
    half = QK_ROPE // 2
    for hd in range(N_HEADS):
        q = qt[hd * QK_HEAD:(hd + 1) * QK_HEAD, :]
        ss = jnp.sum(q * q, axis=0, keepdims=True)
        rinv = lax.rsqrt(ss * (1.0 / QK_HEAD) + EPS) * q_scale
        qg = q * gq
        t1 = qg[QK_NOPE:QK_NOPE + half]
        t2 = qg[QK_NOPE + half:]
        qt_ref[0, hd, 0:QK_NOPE, :] = (qg[:QK_NOPE] * rinv).astype(BF16)
        qt_ref[0, hd, QK_NOPE:QK_NOPE + half, :] = ((t1 * cos - t2 * sin) * rinv).astype(BF16)
        qt_ref[0, hd, QK_NOPE + half:QK_HEAD, :] = ((t2 * cos + t1 * sin) * rinv).astype(BF16)


def _q_proj(h, wlat, glat, wgate, wqt, gq, cos_t, sin_t, tm, q_scale):
    b, s, d = h.shape
    const2 = lambda i, r: (0, 0)
    wb = wgate.shape[1]
    return pl.pallas_call(
        functools.partial(_q_kernel, tm=tm, q_scale=q_scale),
        grid=(b, s // tm),
        in_specs=[pl.BlockSpec((1, tm, d), lambda i, r: (i, r, 0)),
                  _resident(wlat.shape, const2),
                  _resident(glat.shape, const2),
                  _resident(wgate.shape, const2),
                  _resident(wqt.shape, const2),
                  _resident(gq.shape, const2),
                  pl.BlockSpec((1, QK_ROPE // 2, tm), lambda i, r: (i, 0, r)),
                  pl.BlockSpec((1, QK_ROPE // 2, tm), lambda i, r: (i, 0, r))],
        out_specs=[pl.BlockSpec((1, N_HEADS, QK_HEAD, tm), lambda i, r: (i, 0, 0, r)),
                   pl.BlockSpec((1, tm, wb), lambda i, r: (i, r, 0))],
        out_shape=[jax.ShapeDtypeStruct((b, N_HEADS, QK_HEAD, s), BF16),
                   jax.ShapeDtypeStruct((b, s, wb), BF16)],
        compiler_params=_cparams(2),
        name="q_proj",
    )(h, wlat, glat, wgate, wqt, gq, cos_t, sin_t)


def _attn_kernel(qt_ref, k_ref, vt_ref, sg_ref, o_ref, s0_ref, s1_ref, m_ref, l_ref, acc_ref, *, tq, tk):
    qi = pl.program_id(2)
    n_chain = tq // ATT_SUB
    n_per = tq // tk
    sub_per = tk // ATT_SUB
    n_full = qi * n_per
    m_ref[...] = jnp.full(m_ref.shape, -1e30, F32)
    l_ref[...] = jnp.zeros(l_ref.shape, F32)
    acc_ref[...] = jnp.zeros(acc_ref.shape, F32)

    def chain_cols(j):
        return slice(j * ATT_SUB, (j + 1) * ATT_SUB)

    s_refs = (s0_ref, s1_ref)

    def issue_scores(kt, slot, chains):
        start = pl.multiple_of(kt * tk, tk)
        k_tile = k_ref[0, 0, pl.ds(start, tk), :]
        for j in chains:
            s_refs[slot][j] = jnp.dot(k_tile, qt_ref[0, 0, :, chain_cols(j)], preferred_element_type=F32)

    def consume(j, st, vt_tile, mask):
        cols = chain_cols(j)
        if mask is not None:
            st = jnp.where(mask, st, -1e30)
        m_old = m_ref[:, cols]
        m_new = jnp.maximum(m_old, jnp.max(st, axis=0, keepdims=True))
        alpha = jnp.exp2(m_old - m_new)
        p = jnp.exp2(st - m_new)
        l_ref[:, cols] = alpha * l_ref[:, cols] + jnp.sum(p, axis=0, keepdims=True)
        m_ref[:, cols] = m_new
        pv = jnp.dot(vt_tile, p.astype(BF16), preferred_element_type=F32)
        acc_ref[:, cols] = alpha * acc_ref[:, cols] + pv

    def visible_subs(d, j):
        return min(max(j - d * sub_per + 1, 0), sub_per)

    issue_scores(0, 0, range(n_chain))

    def body(pair, carry):
        for slot in range(2):
            kt = 2 * pair + slot
            issue_scores(kt + 1, 1 - slot, range(n_chain))
            vt_tile = vt_ref[0, 0, kt]
            for j in range(n_chain):
                consume(j, s_refs[slot][j], vt_tile, None)
        return carry

    lax.fori_loop(0, n_full // 2, body, 0)

    for d in range(n_per):
        kt = n_full + d
        slot = d % 2
        if d + 1 < n_per:
            issue_scores(kt + 1, 1 - slot, [j for j in range(n_chain) if visible_subs(d + 1, j)])
        for j in range(n_chain):
            n_vis = visible_subs(d, j)
            if n_vis == 0:
                continue
            rows = n_vis * ATT_SUB
            mask = None
            if j < (d + 1) * sub_per:
                k_chunk = lax.broadcasted_iota(jnp.int32, (rows, ATT_SUB), 0) // CHUNK
                q_chunk = lax.broadcasted_iota(jnp.int32, (rows, ATT_SUB), 1) // CHUNK
                mask = k_chunk <= q_chunk + (n_vis - 1) * (ATT_SUB // CHUNK)
            consume(j, s_refs[slot][j, 0:rows, :], vt_ref[0, 0, kt, :, 0:rows], mask)

    o = (acc_ref[...] * (1.0 / l_ref[...])).T
    o_ref[0] = (o * sg_ref[0].astype(F32)).astype(BF16)


def _attention(qt, k, vt, sg, tq, tk):
    b, n_heads, _, s = qt.shape
    wb = sg.shape[2]
    n_kt = s // tk
    assert tq == 2 * tk and tk % ATT_SUB == 0 and s % tq == 0
    return pl.pallas_call(
        functools.partial(_attn_kernel, tq=tq, tk=tk),
        grid=(b, n_heads, s // tq),
        in_specs=[pl.BlockSpec((1, 1, QK_HEAD, tq), lambda i, hd, q: (i, hd, 0, q)),
                  pl.BlockSpec((1, 1, s, QK_HEAD), lambda i, hd, q: (i, hd, 0, 0)),
                  pl.BlockSpec((1, 1, n_kt, V_HEAD, tk), lambda i, hd, q: (i, hd, 0, 0, 0)),
                  pl.BlockSpec((1, tq, V_HEAD), lambda i, hd, q: (i, q, hd))],
        out_specs=pl.BlockSpec((1, tq, V_HEAD), lambda i, hd, q: (i, q, hd)),
        out_shape=jax.ShapeDtypeStruct((b, s, wb), BF16),
        scratch_shapes=[pltpu.VMEM((tq // ATT_SUB, tk, ATT_SUB), F32),
                        pltpu.VMEM((tq // ATT_SUB, tk, ATT_SUB), F32),
                        pltpu.VMEM((1, tq), F32),
                        pltpu.VMEM((1, tq), F32),
                        pltpu.VMEM((V_HEAD, tq), F32)],
        compiler_params=_cparams(3),
        name="attn",
    )(qt, k, vt, sg)


def _rope_tables(positions):
    inv = ROPE_THETA ** (-jnp.arange(0, QK_ROPE, 2, dtype=F32) / QK_ROPE)
    ang = positions.astype(F32)[..., None] * inv
    return jnp.cos(ang), jnp.sin(ang)


def kernel(x, positions, a_norm_g, a_w_in, a_w_group, a_scale, a_w_out, kv_norm_g, kv_w_a, kv_latent_g, kv_w_b, k_norm_g, b_norm_g, b_w_in, b_q_latent_g, b_w_q_b, b_q_norm_g, b_w_out):
    b, s, d = x.shape
    n_a = a_w_in.shape[0]
    n_b = b_w_in.shape[0]
    n_groups, ga = a_w_group.shape[1], a_w_group.shape[2]
    tm = min(512, s)
    tq = min(1024, s)
    tk = min(512, s)

    cos, sin = _rope_tables(positions)
    zeros = jnp.zeros((b, s, LANES - QK_ROPE), F32)
    cos_k = jnp.concatenate([cos, cos, zeros], axis=-1)
    sin_k = jnp.concatenate([-sin, sin, zeros], axis=-1)
    cos_t = jnp.swapaxes(cos, 1, 2)
    sin_t = jnp.swapaxes(sin, 1, 2)

    h = _norm(x, a_norm_g[0], tm)
    for layer in range(n_a):
        w_in = a_w_in[layer].astype(BF16)
        w_group = a_w_group[layer].astype(BF16)
        scale = a_scale[layer].reshape(1, n_groups * ga)
        ys = [_pool_group(h, w_in, w_group, scale, g, tm) for g in range(n_groups)]
        w_out = a_w_out[layer].astype(BF16).reshape(n_groups, ga, d)
        if layer + 1 < n_a:
            x, h = _out_proj(ys, w_out, x, a_norm_g[layer + 1:layer + 2], tm)
        else:
            x, h, h_kv = _out_proj(ys, w_out, x, jnp.stack([b_norm_g[0], kv_norm_g]), tm)

    wa = jnp.pad(kv_w_a, ((0, 0), (0, LANES - QK_ROPE))).astype(BF16)
    kv_w = kv_w_b.reshape(KV_LORA, N_HEADS, QK_NOPE + V_HEAD)
    wk = kv_w[:, :, :QK_NOPE].reshape(KV_LORA, N_HEADS * QK_NOPE).astype(BF16)
    wvt = kv_w[:, :, QK_NOPE:].reshape(KV_LORA, N_HEADS * V_HEAD).T.astype(BF16)
    gkn = k_norm_g[:QK_NOPE].reshape(1, QK_NOPE)
    gkr = jnp.pad(k_norm_g[QK_NOPE:], (0, LANES - QK_ROPE)).reshape(1, LANES)
    k, vt = _shared_kv(h_kv, wa, kv_latent_g.reshape(1, KV_LORA), wk, wvt, gkn, gkr,
                       cos_k, sin_k, tm, tk)

    q_scale = (QK_HEAD ** -0.5) * math.log2(math.e)
    for j in range(n_b):
        w_in = b_w_in[j].astype(BF16)
        qt, sg = _q_proj(h, w_in[:, :Q_LORA], b_q_latent_g[j].reshape(1, Q_LORA), w_in[:, Q_LORA:],
                         b_w_q_b[j].T.astype(BF16), b_q_norm_g[j].reshape(QK_HEAD, 1),
                         cos_t, sin_t, tm, q_scale)
        og = _attention(qt, k, vt, sg, tq, tk)
        w_out = b_w_out[j].astype(BF16)[None]
        if j + 1 < n_b:
            x, h = _out_proj([og], w_out, x, b_norm_g[j + 1:j + 2], tm)
        else:
            (x,) = _out_proj([og], w_out, x, None, tm)
    return x
```

```python
import functools
import math

import jax
import jax.numpy as jnp
from jax import lax
from jax.experimental import pallas as pl
from jax.experimental.pallas import tpu as pltpu

CHUNK = 64
POOL_WINDOWS = (2, 4, 8, 16)
N_HEADS = 16
QK_NOPE = 128
QK_ROPE = 64
QK_HEAD = QK_NOPE + QK_ROPE
V_HEAD = 128
KV_LORA = 512
Q_LORA = 512
ROPE_THETA = 10000.0
EPS = 1e-6

LANES = 128
POOL_HALO = 32
ATT_SUB = 256
VMEM_LIMIT_BYTES = 56 * 1024 * 1024

F32 = jnp.float32
BF16 = jnp.bfloat16
NT_DIMS = (((1,), (1,)), ((), ()))


def _cparams(n_axes):
    return pltpu.CompilerParams(
        dimension_semantics=("arbitrary",) * n_axes,
        vmem_limit_bytes=VMEM_LIMIT_BYTES)


def _resident(shape, index_map):
    return pl.BlockSpec(shape, index_map, pipeline_mode=pl.Buffered(1))


def _silu(v):
    return v * jax.nn.sigmoid(v)


def _norm_kernel(x_ref, g_ref, h_ref):
    x = x_ref[0]
    ms = jnp.mean(x * x, axis=-1, keepdims=True)
    h_ref[0] = (x * lax.rsqrt(ms + EPS) * g_ref[...]).astype(BF16)


def _norm(x, g, tm):
    b, s, d = x.shape
    return pl.pallas_call(
        _norm_kernel,
        grid=(b, s // tm),
        in_specs=[pl.BlockSpec((1, tm, d), lambda i, r: (i, r, 0)),
                  _resident((1, d), lambda i, r: (0, 0))],
        out_specs=pl.BlockSpec((1, tm, d), lambda i, r: (i, r, 0)),
        out_shape=jax.ShapeDtypeStruct((b, s, d), BF16),
        compiler_params=_cparams(2),
        name="norm",
    )(x, g.reshape(1, d))


def _fold_kernel(wu_ref, wgrp_ref, o_ref):
    o_ref[...] = jnp.dot(wu_ref[...].astype(BF16), wgrp_ref[...].astype(BF16),
                         preferred_element_type=F32).astype(BF16)


def _fold_group_weights(w_in, w_group, tr):
    d = w_in.shape[0]
    n_groups, ga, _ = w_group.shape
    return pl.pallas_call(
        _fold_kernel,
        grid=(n_groups, d // tr),
        in_specs=[pl.BlockSpec((tr, ga), lambda g, r: (r, g)),
                  pl.BlockSpec((None, ga, ga), lambda g, r: (g, 0, 0))],
        out_specs=pl.BlockSpec((tr, ga), lambda g, r: (r, g)),
        out_shape=jax.ShapeDtypeStruct((d, n_groups * ga), BF16),
        compiler_params=_cparams(2),
        name="fold_group_weights",
    )(w_in, w_group)


def _pool_kernel(h_ref, wu_ref, wg_ref, sc_ref, y_ref, a_ref, b_ref, c_ref, *, tm, levels):
    r = pl.program_id(1)
    ga = a_ref.shape[1]

    @pl.when(r == 0)
    def _():
        a_ref[0:POOL_HALO, :] = jnp.zeros((POOL_HALO, ga), F32)

    h = h_ref[0]
    a_ref[POOL_HALO:POOL_HALO + tm, :] = jnp.dot(h, wu_ref[...], preferred_element_type=F32)

    src = a_ref
    for j in range(levels):
        shift = 1 << j
        lo = 8 * (j + 1)
        n = POOL_HALO + tm - lo
        dst = (b_ref, c_ref)[j % 2]
        dst[lo:lo + n, :] = src[lo:lo + n, :] + src[lo - shift:lo - shift + n, :]
        src = dst

    window = 1 << levels
    t = r * tm + lax.broadcasted_iota(jnp.int32, (tm, LANES), 0)
    inv_count = 1.0 / jnp.minimum(t + 1, window).astype(F32)
    gate = jnp.dot(h, wg_ref[...], preferred_element_type=F32)
    for c in range(ga // LANES):
        cols = slice(c * LANES, (c + 1) * LANES)
        z = src[POOL_HALO:POOL_HALO + tm, cols] * inv_count - a_ref[POOL_HALO:POOL_HALO + tm, cols]
        y_ref[0, :, cols] = (z * sc_ref[:, cols] * _silu(gate[:, cols])).astype(BF16)

    a_ref[0:POOL_HALO, :] = a_ref[tm:tm + POOL_HALO, :]


def _pool_group(h, w_fold, w_gate, scale, g, tm):
    b, s, d = h.shape
    ga = w_fold.shape[1] // len(POOL_WINDOWS)
    levels = int(math.log2(POOL_WINDOWS[g]))
    assert 1 << levels == POOL_WINDOWS[g] and 8 * levels <= POOL_HALO <= tm
    return pl.pallas_call(
        functools.partial(_pool_kernel, tm=tm, levels=levels),
        grid=(b, s // tm),
        in_specs=[pl.BlockSpec((1, tm, d), lambda i, r: (i, r, 0)),
                  _resident((d, ga), lambda i, r: (0, g)),
                  _resident((d, ga), lambda i, r: (0, g)),
                  _resident((1, ga), lambda i, r: (0, g))],
        out_specs=pl.BlockSpec((1, tm, ga), lambda i, r: (i, r, 0)),
        out_shape=jax.ShapeDtypeStruct((b, s, ga), BF16),
        scratch_shapes=[pltpu.VMEM((POOL_HALO + tm, ga), F32),
                        pltpu.VMEM((POOL_HALO + tm, ga), F32),
                        pltpu.VMEM((POOL_HALO + tm, ga), F32)],
        compiler_params=_cparams(2),
        name=f"pool_g{g}",
    )(h, w_fold, w_gate, scale)


def _out_kernel(*refs, n_parts, n_norm):
    y_refs = refs[:n_parts]
    w_ref, x_ref = refs[n_parts], refs[n_parts + 1]
    g_ref = refs[n_parts + 2] if n_norm else None
    outs = refs[n_parts + 2 + (1 if n_norm else 0):]
    xo_ref, h_refs = outs[0], outs[1:]

    acc = x_ref[0]
    for p in range(n_parts):
        acc = acc + jnp.dot(y_refs[p][0], w_ref[p], preferred_element_type=F32)
    xo_ref[0] = acc
    if n_norm:
        ms = jnp.mean(acc * acc, axis=-1, keepdims=True)
        rn = acc * lax.rsqrt(ms + EPS)
        for j in range(n_norm):
            h_refs[j][0] = (rn * g_ref[j:j + 1, :]).astype(BF16)


def _out_proj(parts, w, x, gains, tm):
    b, s, d = x.shape
    n_parts, kp, _ = w.shape
    n_norm = 0 if gains is None else gains.shape[0]
    row = lambda i, r: (i, r, 0)
    in_specs = [pl.BlockSpec((1, tm, kp), row) for _ in range(n_parts)]
    in_specs += [_resident((n_parts, kp, d), lambda i, r: (0, 0, 0)),
                 pl.BlockSpec((1, tm, d), row)]
    args = list(parts) + [w, x]
    if n_norm:
        in_specs.append(_resident((n_norm, d), lambda i, r: (0, 0)))
        args.append(gains)
    out_specs = [pl.BlockSpec((1, tm, d), row) for _ in range(1 + n_norm)]
    out_shape = [jax.ShapeDtypeStruct((b, s, d), F32)]
    out_shape += [jax.ShapeDtypeStruct((b, s, d), BF16) for _ in range(n_norm)]
    return pl.pallas_call(
        functools.partial(_out_kernel, n_parts=n_parts, n_norm=n_norm),
        grid=(b, s // tm),
        in_specs=in_specs,
        out_specs=out_specs,
        out_shape=out_shape,
        compiler_params=_cparams(2),
        name=f"out_proj_p{n_parts}_n{n_norm}",
    )(*args)


def _kv_kernel(h_ref, wa_ref, glat_ref, wk_ref, wvt_ref, gkn_ref, gkr_ref, cos_ref, sin_ref,
               k_ref, vt_ref, *, tm, tkv):
    h = h_ref[0]
    ckv = jnp.dot(h, wa_ref[...], preferred_element_type=F32)
    c = ckv[:, :KV_LORA]
    kr = ckv[:, KV_LORA:]
    cn = (c * lax.rsqrt(jnp.mean(c * c, axis=-1, keepdims=True) + EPS) * glat_ref[...]).astype(BF16)
    kn = jnp.dot(cn, wk_ref[...], preferred_element_type=F32)
    vt = lax.dot_general(wvt_ref[...], cn, NT_DIMS, preferred_element_type=F32)

    ss_rope = jnp.sum(kr * kr, axis=-1, keepdims=True)
    krg = kr * gkr_ref[...]
    half = QK_ROPE // 2
    swapped = pltpu.roll(krg, half, 1) + pltpu.roll(krg, LANES - half, 1)
    roped = krg * cos_ref[0] + swapped * sin_ref[0]

    for hd in range(N_HEADS):
        knh = kn[:, hd * QK_NOPE:(hd + 1) * QK_NOPE]
        ss = jnp.sum(knh * knh, axis=-1, keepdims=True) + ss_rope
        rinv = lax.rsqrt(ss * (1.0 / QK_HEAD) + EPS)
        k_ref[0, hd, :, 0:QK_NOPE] = (knh * rinv * gkn_ref[...]).astype(BF16)
        k_ref[0, hd, :, QK_NOPE:QK_HEAD] = (roped[:, :QK_ROPE] * rinv).astype(BF16)
        for j in range(tm // tkv):
            vt_ref[0, hd, j] = vt[hd * V_HEAD:(hd + 1) * V_HEAD, j * tkv:(j + 1) * tkv].astype(BF16)


def _shared_kv(h, wa, glat, wk, wvt, gkn, gkr, cos_k, sin_k, tm, tkv):
    b, s, d = h.shape
    row = lambda i, r: (i, r, 0)
    const2 = lambda i, r: (0, 0)
    return pl.pallas_call(
        functools.partial(_kv_kernel, tm=tm, tkv=tkv),
        grid=(b, s // tm),
        in_specs=[pl.BlockSpec((1, tm, d), row),
                  _resident(wa.shape, const2),
                  _resident(glat.shape, const2),
                  _resident(wk.shape, const2),
                  _resident(wvt.shape, const2),
                  _resident(gkn.shape, const2),
                  _resident(gkr.shape, const2),
                  pl.BlockSpec((1, tm, LANES), row),
                  pl.BlockSpec((1, tm, LANES), row)],
        out_specs=[pl.BlockSpec((1, N_HEADS, tm, QK_HEAD), lambda i, r: (i, 0, r, 0)),
                   pl.BlockSpec((1, N_HEADS, tm // tkv, V_HEAD, tkv), lambda i, r: (i, 0, r, 0, 0))],
        out_shape=[jax.ShapeDtypeStruct((b, N_HEADS, s, QK_HEAD), BF16),
                   jax.ShapeDtypeStruct((b, N_HEADS, s // tkv, V_HEAD, tkv), BF16)],
        compiler_params=_cparams(2),
        name="shared_kv",
    )(h, wa, glat, wk, wvt, gkn, gkr, cos_k, sin_k)


def _q_kernel(h_ref, wlat_ref, glat_ref, wgate_ref, wqt_ref, gq_ref, cos_ref, sin_ref,
              qt_ref, sg_ref, *, tm, q_scale):
    h = h_ref[0]
    ql = jnp.dot(h, wlat_ref[...], preferred_element_type=F32)
    qn = (ql * lax.rsqrt(jnp.mean(ql * ql, axis=-1, keepdims=True) + EPS) * glat_ref[...]).astype(BF16)
    gate = jnp.dot(h, wgate_ref[...], preferred_element_type=F32)
    sg_ref[0] = _silu(gate).astype(BF16)

    qt = lax.dot_general(wqt_ref[...], qn, NT_DIMS, preferred_element_type=F32)
    gq = jnp.broadcast_to(gq_ref[...], (QK_HEAD, tm))
    cos = cos_ref[0]
    sin = sin_ref[0]
    half = QK_ROPE // 2
    for hd in range(N_HEADS):
        q = qt[hd * QK_HEAD:(hd + 1) * QK_HEAD, :]
        ss = jnp.sum(q * q, axis=0, keepdims=True)
        rinv = lax.rsqrt(ss * (1.0 / QK_HEAD) + EPS) * q_scale
        qg = q * gq
        t1 = qg[QK_NOPE:QK_NOPE + half]
        t2 = qg[QK_NOPE + half:]
        qt_ref[0, hd, 0:QK_NOPE, :] = (qg[:QK_NOPE] * rinv).astype(BF16)
        qt_ref[0, hd, QK_NOPE:QK_NOPE + half, :] = ((t1 * cos - t2 * sin) * rinv).astype(BF16)
        qt_ref[0, hd, QK_NOPE + half:QK_HEAD, :] = ((t2 * cos + t1 * sin) * rinv).astype(BF16)


def _q_proj(h, wlat, glat, wgate, wqt, gq, cos_t, sin_t, tm, q_scale):
    b, s, d = h.shape
    const2 = lambda i, r: (0, 0)
    wb = wgate.shape[1]
    return pl.pallas_call(
        functools.partial(_q_kernel, tm=tm, q_scale=q_scale),
        grid=(b, s // tm),
        in_specs=[pl.BlockSpec((1, tm, d), lambda i, r: (i, r, 0)),
                  _resident(wlat.shape, const2),
                  _resident(glat.shape, const2),
                  _resident(wgate.shape, const2),
                  _resident(wqt.shape, const2),
                  _resident(gq.shape, const2),
                  pl.BlockSpec((1, QK_ROPE // 2, tm), lambda i, r: (i, 0, r)),
                  pl.BlockSpec((1, QK_ROPE // 2, tm), lambda i, r: (i, 0, r))],
        out_specs=[pl.BlockSpec((1, N_HEADS, QK_HEAD, tm), lambda i, r: (i, 0, 0, r)),
                   pl.BlockSpec((1, tm, wb), lambda i, r: (i, r, 0))],
        out_shape=[jax.ShapeDtypeStruct((b, N_HEADS, QK_HEAD, s), BF16),
                   jax.ShapeDtypeStruct((b, s, wb), BF16)],
        compiler_params=_cparams(2),
        name="q_proj",
    )(h, wlat, glat, wgate, wqt, gq, cos_t, sin_t)


def _attn_kernel(qt_ref, k_ref, vt_ref, sg_ref, o_ref, s0_ref, s1_ref, m_ref, l_ref, acc_ref, *, tq, tk):
    qi = pl.program_id(2)
    n_chain = tq // ATT_SUB
    n_per = tq // tk
    sub_per = tk // ATT_SUB
    n_full = qi * n_per
    m_ref[...] = jnp.full(m_ref.shape, -1e30, F32)
    l_ref[...] = jnp.zeros(l_ref.shape, F32)
    acc_ref[...] = jnp.zeros(acc_ref.shape, F32)

    def chain_cols(j):
        return slice(j * ATT_SUB, (j + 1) * ATT_SUB)

    s_refs = (s0_ref, s1_ref)

    def issue_scores(kt, slot, chains):
        start = pl.multiple_of(kt * tk, tk)
        k_tile = k_ref[0, 0, pl.ds(start, tk), :]
        for j in chains:
            s_refs[slot][j] = jnp.dot(k_tile, qt_ref[0, 0, :, chain_cols(j)], preferred_element_type=F32)

    def consume(j, st, vt_tile, mask):
        cols = chain_cols(j)
        if mask is not None:
            st = jnp.where(mask, st, -1e30)
        m_old = m_ref[:, cols]
        m_new = jnp.maximum(m_old, jnp.max(st, axis=0, keepdims=True))
        alpha = jnp.exp2(m_old - m_new)
        p = jnp.exp2(st - m_new)
        l_ref[:, cols] = alpha * l_ref[:, cols] + jnp.sum(p, axis=0, keepdims=True)
        m_ref[:, cols] = m_new
        pv = jnp.dot(vt_tile, p.astype(BF16), preferred_element_type=F32)
        acc_ref[:, cols] = alpha * acc_ref[:, cols] + pv

    def visible_subs(d, j):
        return min(max(j - d * sub_per + 1, 0), sub_per)

    issue_scores(0, 0, range(n_chain))

    def body(pair, carry):
        for slot in range(2):
            kt = 2 * pair + slot
            issue_scores(kt + 1, 1 - slot, range(n_chain))
            vt_tile = vt_ref[0, 0, kt]
            for j in range(n_chain):
                consume(j, s_refs[slot][j], vt_tile, None)
        return carry

    lax.fori_loop(0, n_full // 2, body, 0)

    for d in range(n_per):
        kt = n_full + d
        slot = d % 2
        if d + 1 < n_per:
            issue_scores(kt + 1, 1 - slot, [j for j in range(n_chain) if visible_subs(d + 1, j)])
        for j in range(n_chain):
            n_vis = visible_subs(d, j)
            if n_vis == 0:
                continue
            rows = n_vis * ATT_SUB
            mask = None
            if j < (d + 1) * sub_per:
                k_chunk = lax.broadcasted_iota(jnp.int32, (rows, ATT_SUB), 0) // CHUNK
                q_chunk = lax.broadcasted_iota(jnp.int32, (rows, ATT_SUB), 1) // CHUNK
                mask = k_chunk <= q_chunk + (n_vis - 1) * (ATT_SUB // CHUNK)
            consume(j, s_refs[slot][j, 0:rows, :], vt_ref[0, 0, kt, :, 0:rows], mask)

    o = (acc_ref[...] * (1.0 / l_ref[...])).T
    o_ref[0] = (o * sg_ref[0].astype(F32)).astype(BF16)


def _attention(qt, k, vt, sg, tq, tk):
    b, n_heads, _, s = qt.shape
    wb = sg.shape[2]
    n_kt = s // tk
    assert tq == 2 * tk and tk % ATT_SUB == 0 and s % tq == 0
    return pl.pallas_call(
        functools.partial(_attn_kernel, tq=tq, tk=tk),
        grid=(b, n_heads, s // tq),
        in_specs=[pl.BlockSpec((1, 1, QK_HEAD, tq), lambda i, hd, q: (i, hd, 0, q)),
                  pl.BlockSpec((1, 1, s, QK_HEAD), lambda i, hd, q: (i, hd, 0, 0)),
                  pl.BlockSpec((1, 1, n_kt, V_HEAD, tk), lambda i, hd, q: (i, hd, 0, 0, 0)),
                  pl.BlockSpec((1, tq, V_HEAD), lambda i, hd, q: (i, q, hd))],
        out_specs=pl.BlockSpec((1, tq, V_HEAD), lambda i, hd, q: (i, q, hd)),
        out_shape=jax.ShapeDtypeStruct((b, s, wb), BF16),
        scratch_shapes=[pltpu.VMEM((tq // ATT_SUB, tk, ATT_SUB), F32),
                        pltpu.VMEM((tq // ATT_SUB, tk, ATT_SUB), F32),
                        pltpu.VMEM((1, tq), F32),
                        pltpu.VMEM((1, tq), F32),
                        pltpu.VMEM((V_HEAD, tq), F32)],
        compiler_params=_cparams(3),
        name="attn",
    )(qt, k, vt, sg)


def _rope_tables(positions):
    inv = ROPE_THETA ** (-jnp.arange(0, QK_ROPE, 2, dtype=F32) / QK_ROPE)
    ang = positions.astype(F32)[..., None] * inv
    return jnp.cos(ang), jnp.sin(ang)


def kernel(x, positions, a_norm_g, a_w_in, a_w_group, a_scale, a_w_out, kv_norm_g, kv_w_a, kv_latent_g, kv_w_b, k_norm_g, b_norm_g, b_w_in, b_q_latent_g, b_w_q_b, b_q_norm_g, b_w_out):
    b, s, d = x.shape
    n_a = a_w_in.shape[0]
    n_b = b_w_in.shape[0]
    n_groups, ga = a_w_group.shape[1], a_w_group.shape[2]
    tm = min(512, s)
    tq = min(1024, s)
    tk = min(512, s)

    cos, sin = _rope_tables(positions)
    zeros = jnp.zeros((b, s, LANES - QK_ROPE), F32)
    cos_k = jnp.concatenate([cos, cos, zeros], axis=-1)
    sin_k = jnp.concatenate([-sin, sin, zeros], axis=-1)
    cos_t = jnp.swapaxes(cos, 1, 2)
    sin_t = jnp.swapaxes(sin, 1, 2)

    h = _norm(x, a_norm_g[0], tm)
    for layer in range(n_a):
        w_fold = _fold_group_weights(a_w_in[layer], a_w_group[layer], min(1024, d))
        w_gate = a_w_in[layer][:, n_groups * ga:].astype(BF16)
        scale = a_scale[layer].reshape(1, n_groups * ga)
        ys = [_pool_group(h, w_fold, w_gate, scale, g, tm) for g in range(n_groups)]
        w_out = a_w_out[layer].astype(BF16).reshape(n_groups, ga, d)
        if layer + 1 < n_a:
            x, h = _out_proj(ys, w_out, x, a_norm_g[layer + 1:layer + 2], tm)
        else:
            x, h, h_kv = _out_proj(ys, w_out, x, jnp.stack([b_norm_g[0], kv_norm_g]), tm)

    wa = jnp.pad(kv_w_a, ((0, 0), (0, LANES - QK_ROPE))).astype(BF16)
    kv_w = kv_w_b.reshape(KV_LORA, N_HEADS, QK_NOPE + V_HEAD)
    wk = kv_w[:, :, :QK_NOPE].reshape(KV_LORA, N_HEADS * QK_NOPE).astype(BF16)
    wvt = kv_w[:, :, QK_NOPE:].reshape(KV_LORA, N_HEADS * V_HEAD).T.astype(BF16)
    gkn = k_norm_g[:QK_NOPE].reshape(1, QK_NOPE)
    gkr = jnp.pad(k_norm_g[QK_NOPE:], (0, LANES - QK_ROPE)).reshape(1, LANES)
    k, vt = _shared_kv(h_kv, wa, kv_latent_g.reshape(1, KV_LORA), wk, wvt, gkn, gkr,
                       cos_k, sin_k, tm, tk)

    q_scale = (QK_HEAD ** -0.5) * math.log2(math.e)
    for j in range(n_b):
        w_in = b_w_in[j].astype(BF16)
        qt, sg = _q_proj(h, w_in[:, :Q_LORA], b_q_latent_g[j].reshape(1, Q_LORA), w_in[:, Q_LORA:],
                         b_w_q_b[j].T.astype(BF16), b_q_norm_g[j].reshape(QK_HEAD, 1),
                         cos_t, sin_t, tm, q_scale)
        og = _attention(qt, k, vt, sg, tq, tk)
        w_out = b_w_out[j].astype(BF16)[None]
        if j + 1 < n_b:
            x, h = _out_proj([og], w_out, x, b_norm_g[j + 1:j + 2], tm)
        else:
            (x,) = _out_proj([og], w_out, x, None, tm)
    return x
```

```python
import functools
import math

import jax
import jax.numpy as jnp
from jax import lax
from jax.experimental import pallas as pl
from jax.experimental.pallas import tpu as pltpu

CHUNK = 64
POOL_WINDOWS = (2, 4, 8, 16)
N_HEADS = 16
QK_NOPE = 128
QK_ROPE = 64
QK_HEAD = QK_NOPE + QK_ROPE
V_HEAD = 128
KV_LORA = 512
Q_LORA = 512
ROPE_THETA = 10000.0
EPS = 1e-6

LANES = 128
POOL_HALO = 32
ATT_SUB = 256
VMEM_LIMIT_BYTES = 56 * 1024 * 1024

F32 = jnp.float32
BF16 = jnp.bfloat16
NT_DIMS = (((1,), (1,)), ((), ()))


def _cparams(n_axes):
    return pltpu.CompilerParams(
        dimension_semantics=("arbitrary",) * n_axes,
        vmem_limit_bytes=VMEM_LIMIT_BYTES)


def _resident(shape, index_map):
    return pl.BlockSpec(shape, index_map, pipeline_mode=pl.Buffered(1))


def _silu(v):
    return v * jax.nn.sigmoid(v)


def _norm_kernel(x_ref, g_ref, h_ref):
    x = x_ref[0]
    ms = jnp.mean(x * x, axis=-1, keepdims=True)
    h_ref[0] = (x * lax.rsqrt(ms + EPS) * g_ref[...]).astype(BF16)


def _norm(x, g, tm):
    b, s, d = x.shape
    return pl.pallas_call(
        _norm_kernel,
        grid=(b, s // tm),
        in_specs=[pl.BlockSpec((1, tm, d), lambda i, r: (i, r, 0)),
                  _resident((1, d), lambda i, r: (0, 0))],
        out_specs=pl.BlockSpec((1, tm, d), lambda i, r: (i, r, 0)),
        out_shape=jax.ShapeDtypeStruct((b, s, d), BF16),
        compiler_params=_cparams(2),
        name="norm",
    )(x, g.reshape(1, d))


def _fold_kernel(wu_ref, wgrp_ref, o_ref):
    o_ref[...] = jnp.dot(wu_ref[...].astype(BF16), wgrp_ref[...].astype(BF16),
                         preferred_element_type=F32).astype(BF16)


def _fold_group_weights(w_in, w_group, tr):
    d = w_in.shape[0]
    n_groups, ga, _ = w_group.shape
    return pl.pallas_call(
        _fold_kernel,
        grid=(n_groups, d // tr),
        in_specs=[pl.BlockSpec((tr, ga), lambda g, r: (r, g)),
                  pl.BlockSpec((None, ga, ga), lambda g, r: (g, 0, 0))],
        out_specs=pl.BlockSpec((tr, ga), lambda g, r: (r, g)),
        out_shape=jax.ShapeDtypeStruct((d, n_groups * ga), BF16),
        compiler_params=_cparams(2),
        name="fold_group_weights",
    )(w_in, w_group)


def _pool_kernel(h_ref, wu_ref, wg_ref, sc_ref, y_ref, a_ref, b_ref, c_ref, *, tm, levels):
    r = pl.program_id(1)
    ga = a_ref.shape[1]

    @pl.when(r == 0)
    def _():
        a_ref[0:POOL_HALO, :] = jnp.zeros((POOL_HALO, ga), F32)

    h = h_ref[0]
    a_ref[POOL_HALO:POOL_HALO + tm, :] = jnp.dot(h, wu_ref[...], preferred_element_type=F32)

    src = a_ref
    for j in range(levels):
        shift = 1 << j
        lo = 8 * (j + 1)
        n = POOL_HALO + tm - lo
        dst = (b_ref, c_ref)[j % 2]
        dst[lo:lo + n, :] = src[lo:lo + n, :] + src[lo - shift:lo - shift + n, :]
        src = dst

    window = 1 << levels
    t = r * tm + lax.broadcasted_iota(jnp.int32, (tm, LANES), 0)
    inv_count = 1.0 / jnp.minimum(t + 1, window).astype(F32)
    gate = jnp.dot(h, wg_ref[...], preferred_element_type=F32)
    for c in range(ga // LANES):
        cols = slice(c * LANES, (c + 1) * LANES)
        z = src[POOL_HALO:POOL_HALO + tm, cols] * inv_count - a_ref[POOL_HALO:POOL_HALO + tm, cols]
        y_ref[0, :, cols] = (z * sc_ref[:, cols] * _silu(gate[:, cols])).astype(BF16)

    a_ref[0:POOL_HALO, :] = a_ref[tm:tm + POOL_HALO, :]


def _pool_group(h, w_fold, w_gate, scale, g, tm):
    b, s, d = h.shape
    ga = w_fold.shape[1] // len(POOL_WINDOWS)
    levels = int(math.log2(POOL_WINDOWS[g]))
    assert 1 << levels == POOL_WINDOWS[g] and 8 * levels <= POOL_HALO <= tm
    return pl.pallas_call(
        functools.partial(_pool_kernel, tm=tm, levels=levels),
        grid=(b, s // tm),
        in_specs=[pl.BlockSpec((1, tm, d), lambda i, r: (i, r, 0)),
                  _resident((d, ga), lambda i, r: (0, g)),
                  _resident((d, ga), lambda i, r: (0, g)),
                  _resident((1, ga), lambda i, r: (0, g))],
        out_specs=pl.BlockSpec((1, tm, ga), lambda i, r: (i, r, 0)),
        out_shape=jax.ShapeDtypeStruct((b, s, ga), BF16),
        scratch_shapes=[pltpu.VMEM((POOL_HALO + tm, ga), F32),
                        pltpu.VMEM((POOL_HALO + tm, ga), F32),
                        pltpu.VMEM((POOL_HALO + tm, ga), F32)],
        compiler_params=_cparams(2),
        name=f"pool_g{g}",
    )(h, w_fold, w_gate, scale)


def _out_kernel(*refs, n_parts, n_norm):
    y_refs = refs[:n_parts]
    w_ref, x_ref = refs[n_parts], refs[n_parts + 1]
    g_ref = refs[n_parts + 2] if n_norm else None
    outs = refs[n_parts + 2 + (1 if n_norm else 0):]
    xo_ref, h_refs = outs[0], outs[1:]

    acc = x_ref[0]
    for p in range(n_parts):
        acc = acc + jnp.dot(y_refs[p][0], w_ref[p], preferred_element_type=F32)
    xo_ref[0] = acc
    if n_norm:
        ms = jnp.mean(acc * acc, axis=-1, keepdims=True)
        rn = acc * lax.rsqrt(ms + EPS)
        for j in range(n_norm):
            h_refs[j][0] = (rn * g_ref[j:j + 1, :]).astype(BF16)


def _out_proj(parts, w, x, gains, tm):
    b, s, d = x.shape
    n_parts, kp, _ = w.shape
    n_norm = 0 if gains is None else gains.shape[0]
    row = lambda i, r: (i, r, 0)
    in_specs = [pl.BlockSpec((1, tm, kp), row) for _ in range(n_parts)]
    in_specs += [_resident((n_parts, kp, d), lambda i, r: (0, 0, 0)),
                 pl.BlockSpec((1, tm, d), row)]
    args = list(parts) + [w, x]
    if n_norm:
        in_specs.append(_resident((n_norm, d), lambda i, r: (0, 0)))
        args.append(gains)
    out_specs = [pl.BlockSpec((1, tm, d), row) for _ in range(1 + n_norm)]
    out_shape = [jax.ShapeDtypeStruct((b, s, d), F32)]
    out_shape += [jax.ShapeDtypeStruct((b, s, d), BF16) for _ in range(n_norm)]
    return pl.pallas_call(
        functools.partial(_out_kernel, n_parts=n_parts, n_norm=n_norm),
        grid=(b, s // tm),
        in_specs=in_specs,
        out_specs=out_specs,
        out_shape=out_shape,
        compiler_params=_cparams(2),
        name=f"out_proj_p{n_parts}_n{n_norm}",
    )(*args)


def _kv_kernel(h_ref, wa_ref, glat_ref, wk_ref, wvt_ref, gkn_ref, gkr_ref, cos_ref, sin_ref,
               k_ref, vt_ref, *, tm, tkv):
    h = h_ref[0]
    ckv = jnp.dot(h, wa_ref[...], preferred_element_type=F32)
    c = ckv[:, :KV_LORA]
    kr = ckv[:, KV_LORA:]
    cn = (c * lax.rsqrt(jnp.mean(c * c, axis=-1, keepdims=True) + EPS) * glat_ref[...]).astype(BF16)
    kn = jnp.dot(cn, wk_ref[...], preferred_element_type=F32)
    vt = lax.dot_general(wvt_ref[...], cn, NT_DIMS, preferred_element_type=F32)

    ss_rope = jnp.sum(kr * kr, axis=-1, keepdims=True)
    krg = kr * gkr_ref[...]
    half = QK_ROPE // 2
    swapped = pltpu.roll(krg, half, 1) + pltpu.roll(krg, LANES - half, 1)
    roped = krg * cos_ref[0] + swapped * sin_ref[0]

    for hd in range(N_HEADS):
        knh = kn[:, hd * QK_NOPE:(hd + 1) * QK_NOPE]
        ss = jnp.sum(knh * knh, axis=-1, keepdims=True) + ss_rope
        rinv = lax.rsqrt(ss * (1.0 / QK_HEAD) + EPS)
        k_ref[0, hd, :, 0:QK_NOPE] = (knh * rinv * gkn_ref[...]).astype(BF16)
        k_ref[0, hd, :, QK_NOPE:QK_HEAD] = (roped[:, :QK_ROPE] * rinv).astype(BF16)
        for j in range(tm // tkv):
            vt_ref[0, hd, j] = vt[hd * V_HEAD:(hd + 1) * V_HEAD, j * tkv:(j + 1) * tkv].astype(BF16)


def _shared_kv(h, wa, glat, wk, wvt, gkn, gkr, cos_k, sin_k, tm, tkv):
    b, s, d = h.shape
    row = lambda i, r: (i, r, 0)
    const2 = lambda i, r: (0, 0)
    return pl.pallas_call(
        functools.partial(_kv_kernel, tm=tm, tkv=tkv),
        grid=(b, s // tm),
        in_specs=[pl.BlockSpec((1, tm, d), row),
                  _resident(wa.shape, const2),
                  _resident(glat.shape, const2),
                  _resident(wk.shape, const2),
                  _resident(wvt.shape, const2),
                  _resident(gkn.shape, const2),
                  _resident(gkr.shape, const2),
                  pl.BlockSpec((1, tm, LANES), row),
                  pl.BlockSpec((1, tm, LANES), row)],
        out_specs=[pl.BlockSpec((1, N_HEADS, tm, QK_HEAD), lambda i, r: (i, 0, r, 0)),
                   pl.BlockSpec((1, N_HEADS, tm // tkv, V_HEAD, tkv), lambda i, r: (i, 0, r, 0, 0))],
        out_shape=[jax.ShapeDtypeStruct((b, N_HEADS, s, QK_HEAD), BF16),
                   jax.ShapeDtypeStruct((b, N_HEADS, s // tkv, V_HEAD, tkv), BF16)],
        compiler_params=_cparams(2),
        name="shared_kv",
    )(h, wa, glat, wk, wvt, gkn, gkr, cos_k, sin_k)


def _q_kernel(h_ref, wlat_ref, glat_ref, wgate_ref, wqt_ref, gq_ref, cos_ref, sin_ref,
              qt_ref, sg_ref, *, tm, q_scale):
    h = h_ref[0]
    ql = jnp.dot(h, wlat_ref[...], preferred_element_type=F32)
    qn = (ql * lax.rsqrt(jnp.mean(ql * ql, axis=-1, keepdims=True) + EPS) * glat_ref[...]).astype(BF16)
    gate = jnp.dot(h, wgate_ref[...], preferred_element_type=F32)
    sg_ref[0] = _silu(gate).astype(BF16)

    qt = lax.dot_general(wqt_ref[...], qn, NT_DIMS, preferred_element_type=F32)
    gq = jnp.broadcast_to(gq_ref[...], (QK_HEAD, tm))
    cos = cos_ref[0]
    sin = sin_ref[0]
    half = QK_ROPE // 2
    for hd in range(N_HEADS):
        q = qt[hd * QK_HEAD:(hd + 1) * QK_HEAD, :]
        ss = jnp.sum(q * q, axis=0, keepdims=True)
        rinv = lax.rsqrt(ss * (1.0 / QK_HEAD) + EPS) * q_scale
        qg = q * gq
        t1 = qg[QK_NOPE:QK_NOPE + half]
        t2 = qg[QK_NOPE + half:]
        qt_ref[0, hd, 0, 0:QK_NOPE, :] = (qg[:QK_NOPE] * rinv).astype(BF16)
        qt_ref[0, hd, 0, QK_NOPE:QK_NOPE + half, :] = ((t1 * cos - t2 * sin) * rinv).astype(BF16)
        qt_ref[0, hd, 0, QK_NOPE + half:QK_HEAD, :] = ((t2 * cos + t1 * sin) * rinv).astype(BF16)


def _q_proj(h, wlat, glat, wgate, wqt, gq, cos_t, sin_t, tm, tq, q_scale):
    b, s, d = h.shape
    per_q = tq // tm
    const2 = lambda i, r: (0, 0)
    wb = wgate.shape[1]
    return pl.pallas_call(
        functools.partial(_q_kernel, tm=tm, q_scale=q_scale),
        grid=(b, s // tm),
        in_specs=[pl.BlockSpec((1, tm, d), lambda i, r: (i, r, 0)),
                  _resident(wlat.shape, const2),
                  _resident(glat.shape, const2),
                  _resident(wgate.shape, const2),
                  _resident(wqt.shape, const2),
                  _resident(gq.shape, const2),
                  pl.BlockSpec((1, QK_ROPE // 2, tm), lambda i, r: (i, 0, r)),
                  pl.BlockSpec((1, QK_ROPE // 2, tm), lambda i, r: (i, 0, r))],
        out_specs=[pl.BlockSpec((1, N_HEADS, 1, QK_HEAD, tm), lambda i, r: (i, 0, r // per_q, 0, r % per_q)),
                   pl.BlockSpec((1, tm, wb), lambda i, r: (i, r, 0))],
        out_shape=[jax.ShapeDtypeStruct((b, N_HEADS, s // tq, QK_HEAD, tq), BF16),
                   jax.ShapeDtypeStruct((b, s, wb), BF16)],
        compiler_params=_cparams(2),
        name="q_proj",
    )(h, wlat, glat, wgate, wqt, gq, cos_t, sin_t)


def _attn_kernel(qt_ref, k_ref, vt_ref, sg_ref, o_ref, s0_ref, s1_ref, m_ref, l_ref, acc_ref, *, tq, tk):
    n_q = qt_ref.shape[2]
    n_chain = tq // ATT_SUB
    n_per = tq // tk
    sub_per = tk // ATT_SUB
    s_refs = (s0_ref, s1_ref)

    def chain_cols(j):
        return slice(j * ATT_SUB, (j + 1) * ATT_SUB)

    items = []
    for qi in range(n_q):
        items += [(qi, kt, None) for kt in range(qi * n_per)]
        items += [(qi, qi * n_per + d, d) for d in range(n_per)]

    def visible_rows(item, j):
        _, _, d = item
        if d is None:
            return tk
        return min(max(j - d * sub_per + 1, 0), sub_per) * ATT_SUB

    def issue_scores(t):
        qi, kt, _ = items[t]
        for j in range(n_chain):
            rows = visible_rows(items[t], j)
            if rows:
                k_tile = k_ref[0, 0, kt * tk:kt * tk + rows, :]
                s_refs[t % 2][j, 0:rows, :] = jnp.dot(
                    k_tile, qt_ref[0, 0, qi, :, chain_cols(j)], preferred_element_type=F32)

    def consume(qi, j, st, vt_tile, mask):
        cols = chain_cols(j)
        if mask is not None:
            st = jnp.where(mask, st, -1e30)
        m_old = m_ref[qi, :, cols]
        m_new = jnp.maximum(m_old, jnp.max(st, axis=0, keepdims=True))
        alpha = jnp.exp2(m_old - m_new)
        p = jnp.exp2(st - m_new)
        l_ref[qi, :, cols] = alpha * l_ref[qi, :, cols] + jnp.sum(p, axis=0, keepdims=True)
        m_ref[qi, :, cols] = m_new
        pv = jnp.dot(vt_tile, p.astype(BF16), preferred_element_type=F32)
        acc_ref[qi, :, cols] = alpha * acc_ref[qi, :, cols] + pv

    issue_scores(0)
    for t, (qi, kt, d) in enumerate(items):
        if t == 0 or items[t - 1][0] != qi:
            m_ref[qi] = jnp.full(m_ref.shape[1:], -1e30, F32)
            l_ref[qi] = jnp.zeros(l_ref.shape[1:], F32)
            acc_ref[qi] = jnp.zeros(acc_ref.shape[1:], F32)
        if t + 1 < len(items):
            issue_scores(t + 1)
        for j in range(n_chain):
            rows = visible_rows(items[t], j)
            if rows == 0:
                continue
            mask = None
            if d is not None and j < (d + 1) * sub_per:
                k_chunk = lax.broadcasted_iota(jnp.int32, (rows, ATT_SUB), 0) // CHUNK
                q_chunk = lax.broadcasted_iota(jnp.int32, (rows, ATT_SUB), 1) // CHUNK
                mask = k_chunk <= q_chunk + (rows // ATT_SUB - 1) * (ATT_SUB // CHUNK)
            consume(qi, j, s_refs[t % 2][j, 0:rows, :], vt_ref[0, 0, kt, :, 0:rows], mask)
        if t + 1 == len(items) or items[t + 1][0] != qi:
            o = (acc_ref[qi] * (1.0 / l_ref[qi])).T
            rows_q = slice(qi * tq, (qi + 1) * tq)
            o_ref[0, rows_q, :] = (o * sg_ref[0, rows_q, :].astype(F32)).astype(BF16)


def _attention(qt, k, vt, sg, tk):
    b, n_heads, n_q, _, tq = qt.shape
    s = n_q * tq
    wb = sg.shape[2]
    n_kt = s // tk
    assert tq % tk == 0 and tk % ATT_SUB == 0
    return pl.pallas_call(
        functools.partial(_attn_kernel, tq=tq, tk=tk),
        grid=(b, n_heads),
        in_specs=[pl.BlockSpec((1, 1, n_q, QK_HEAD, tq), lambda i, hd: (i, hd, 0, 0, 0)),
                  pl.BlockSpec((1, 1, s, QK_HEAD), lambda i, hd: (i, hd, 0, 0)),
                  pl.BlockSpec((1, 1, n_kt, V_HEAD, tk), lambda i, hd: (i, hd, 0, 0, 0)),
                  pl.BlockSpec((1, s, V_HEAD), lambda i, hd: (i, 0, hd))],
        out_specs=pl.BlockSpec((1, s, V_HEAD), lambda i, hd: (i, 0, hd)),
        out_shape=jax.ShapeDtypeStruct((b, s, wb), BF16),
        scratch_shapes=[pltpu.VMEM((tq // ATT_SUB, tk, ATT_SUB), F32),
                        pltpu.VMEM((tq // ATT_SUB, tk, ATT_SUB), F32),
                        pltpu.VMEM((n_q, 1, tq), F32),
                        pltpu.VMEM((n_q, 1, tq), F32),
                        pltpu.VMEM((n_q, V_HEAD, tq), F32)],
        compiler_params=_cparams(2),
        name="attn",
    )(qt, k, vt, sg)


def _rope_tables(positions):
    inv = ROPE_THETA ** (-jnp.arange(0, QK_ROPE, 2, dtype=F32) / QK_ROPE)
    ang = positions.astype(F32)[..., None] * inv
    return jnp.cos(ang), jnp.sin(ang)


def kernel(x, positions, a_norm_g, a_w_in, a_w_group, a_scale, a_w_out, kv_norm_g, kv_w_a, kv_latent_g, kv_w_b, k_norm_g, b_norm_g, b_w_in, b_q_latent_g, b_w_q_b, b_q_norm_g, b_w_out):
    b, s, d = x.shape
    n_a = a_w_in.shape[0]
    n_b = b_w_in.shape[0]
    n_groups, ga = a_w_group.shape[1], a_w_group.shape[2]
    tm = min(512, s)
    tq = min(1024, s)
    tk = min(512, s)

    cos, sin = _rope_tables(positions)
    zeros = jnp.zeros((b, s, LANES - QK_ROPE), F32)
    cos_k = jnp.concatenate([cos, cos, zeros], axis=-1)
    sin_k = jnp.concatenate([-sin, sin, zeros], axis=-1)
    cos_t = jnp.swapaxes(cos, 1, 2)
    sin_t = jnp.swapaxes(sin, 1, 2)

    h = _norm(x, a_norm_g[0], tm)
    for layer in range(n_a):
        w_fold = _fold_group_weights(a_w_in[layer], a_w_group[layer], min(1024, d))
        w_gate = a_w_in[layer][:, n_groups * ga:].astype(BF16)
        scale = a_scale[layer].reshape(1, n_groups * ga)
        ys = [_pool_group(h, w_fold, w_gate, scale, g, tm) for g in range(n_groups)]
        w_out = a_w_out[layer].astype(BF16).reshape(n_groups, ga, d)
        if layer + 1 < n_a:
            x, h = _out_proj(ys, w_out, x, a_norm_g[layer + 1:layer + 2], tm)
        else:
            x, h, h_kv = _out_proj(ys, w_out, x, jnp.stack([b_norm_g[0], kv_norm_g]), tm)

    wa = jnp.pad(kv_w_a, ((0, 0), (0, LANES - QK_ROPE))).astype(BF16)
    kv_w = kv_w_b.reshape(KV_LORA, N_HEADS, QK_NOPE + V_HEAD)
    wk = kv_w[:, :, :QK_NOPE].reshape(KV_LORA, N_HEADS * QK_NOPE).astype(BF16)
    wvt = kv_w[:, :, QK_NOPE:].reshape(KV_LORA, N_HEADS * V_HEAD).T.astype(BF16)
    gkn = k_norm_g[:QK_NOPE].reshape(1, QK_NOPE)
    gkr = jnp.pad(k_norm_g[QK_NOPE:], (0, LANES - QK_ROPE)).reshape(1, LANES)
    k, vt = _shared_kv(h_kv, wa, kv_latent_g.reshape(1, KV_LORA), wk, wvt, gkn, gkr,
                       cos_k, sin_k, tm, tk)

    q_scale = (QK_HEAD ** -0.5) * math.log2(math.e)
    for j in range(n_b):
        w_in = b_w_in[j].astype(BF16)
        qt, sg = _q_proj(h, w_in[:, :Q_LORA], b_q_latent_g[j].reshape(1, Q_LORA), w_in[:, Q_LORA:],
                         b_w_q_b[j].T.astype(BF16), b_q_norm_g[j].reshape(QK_HEAD, 1),
                         cos_t, sin_t, tm, tq, q_scale)
        og = _attention(qt, k, vt, sg, tk)
        w_out = b_w_out[j].astype(BF16)[None]
        if j + 1 < n_b:
            x, h = _out_proj([og], w_out, x, b_norm_g[j + 1:j + 2], tm)
        else:
            (x,) = _out_proj([og], w_out, x, None, tm)
    return x
```

```python
import functools
import math

import jax
import jax.numpy as jnp
from jax import lax
from jax.experimental import pallas as pl
from jax.experimental.pallas import tpu as pltpu

CHUNK = 64
POOL_WINDOWS = (2, 4, 8, 16)
N_HEADS = 16
QK_NOPE = 128
QK_ROPE = 64
QK_HEAD = QK_NOPE + QK_ROPE
V_HEAD = 128
KV_LORA = 512
Q_LORA = 512
ROPE_THETA = 10000.0
EPS = 1e-6

LANES = 128
POOL_HALO = 32
ATT_SUB = 256
V_ONES = 16
V_ROWS = V_HEAD + V_ONES
VMEM_LIMIT_BYTES = 56 * 1024 * 1024

F32 = jnp.float32
BF16 = jnp.bfloat16
NT_DIMS = (((1,), (1,)), ((), ()))


def _cparams(n_axes):
    return pltpu.CompilerParams(
        dimension_semantics=("arbitrary",) * n_axes,
        vmem_limit_bytes=VMEM_LIMIT_BYTES)


def _resident(shape, index_map):
    return pl.BlockSpec(shape, index_map, pipeline_mode=pl.Buffered(1))


def _silu(v):
    return v * jax.nn.sigmoid(v)


def _norm_kernel(x_ref, g_ref, h_ref):
    x = x_ref[0]
    ms = jnp.mean(x * x, axis=-1, keepdims=True)
    h_ref[0] = (x * lax.rsqrt(ms + EPS) * g_ref[...]).astype(BF16)


def _norm(x, g, tm):
    b, s, d = x.shape
    return pl.pallas_call(
        _norm_kernel,
        grid=(b, s // tm),
        in_specs=[pl.BlockSpec((1, tm, d), lambda i, r: (i, r, 0)),
                  _resident((1, d), lambda i, r: (0, 0))],
        out_specs=pl.BlockSpec((1, tm, d), lambda i, r: (i, r, 0)),
        out_shape=jax.ShapeDtypeStruct((b, s, d), BF16),
        compiler_params=_cparams(2),
        name="norm",
    )(x, g.reshape(1, d))


def _fold_kernel(wu_ref, wgrp_ref, o_ref):
    o_ref[...] = jnp.dot(wu_ref[...].astype(BF16), wgrp_ref[...].astype(BF16),
                         preferred_element_type=F32).astype(BF16)


def _fold_group_weights(w_in, w_group, layer, tr):
    d = w_in.shape[1]
    _, n_groups, ga, _ = w_group.shape
    return pl.pallas_call(
        _fold_kernel,
        grid=(n_groups, d // tr),
        in_specs=[pl.BlockSpec((None, tr, ga), lambda g, r: (layer, r, g)),
                  pl.BlockSpec((None, None, ga, ga), lambda g, r: (layer, g, 0, 0))],
        out_specs=pl.BlockSpec((tr, ga), lambda g, r: (r, g)),
        out_shape=jax.ShapeDtypeStruct((d, n_groups * ga), BF16),
        compiler_params=_cparams(2),
        name="fold_group_weights",
    )(w_in, w_group)


def _pool_kernel(h_ref, wu_ref, wg_ref, sc_ref, y_ref, a_ref, b_ref, c_ref, *, tm, levels):
    r = pl.program_id(1)
    ga = a_ref.shape[1]

    @pl.when(r == 0)
    def _():
        a_ref[0:POOL_HALO, :] = jnp.zeros((POOL_HALO, ga), F32)

    h = h_ref[0]
    a_ref[POOL_HALO:POOL_HALO + tm, :] = jnp.dot(h, wu_ref[...], preferred_element_type=F32)

    src = a_ref
    for j in range(levels):
        shift = 1 << j
        lo = 8 * (j + 1)
        n = POOL_HALO + tm - lo
        dst = (b_ref, c_ref)[j % 2]
        dst[lo:lo + n, :] = src[lo:lo + n, :] + src[lo - shift:lo - shift + n, :]
        src = dst

    window = 1 << levels
    t = r * tm + lax.broadcasted_iota(jnp.int32, (tm, LANES), 0)
    inv_count = 1.0 / jnp.minimum(t + 1, window).astype(F32)
    gate = jnp.dot(h, wg_ref[...], preferred_element_type=F32)
    for c in range(ga // LANES):
        cols = slice(c * LANES, (c + 1) * LANES)
        z = src[POOL_HALO:POOL_HALO + tm, cols] * inv_count - a_ref[POOL_HALO:POOL_HALO + tm, cols]
        y_ref[0, :, cols] = (z * sc_ref[:, cols] * _silu(gate[:, cols])).astype(BF16)

    a_ref[0:POOL_HALO, :] = a_ref[tm:tm + POOL_HALO, :]


def _pool_group(h, w_fold, w_gate, scale, layer, g, tm):
    b, s, d = h.shape
    ga = w_fold.shape[1] // len(POOL_WINDOWS)
    levels = int(math.log2(POOL_WINDOWS[g]))
    assert 1 << levels == POOL_WINDOWS[g] and 8 * levels <= POOL_HALO <= tm
    return pl.pallas_call(
        functools.partial(_pool_kernel, tm=tm, levels=levels),
        grid=(b, s // tm),
        in_specs=[pl.BlockSpec((1, tm, d), lambda i, r: (i, r, 0)),
                  _resident((d, ga), lambda i, r: (0, g)),
                  _resident((None, d, ga), lambda i, r: (layer, 0, g)),
                  _resident((None, 1, ga), lambda i, r: (layer, 0, g))],
        out_specs=pl.BlockSpec((1, tm, ga), lambda i, r: (i, r, 0)),
        out_shape=jax.ShapeDtypeStruct((b, s, ga), BF16),
        scratch_shapes=[pltpu.VMEM((POOL_HALO + tm, ga), F32),
                        pltpu.VMEM((POOL_HALO + tm, ga), F32),
                        pltpu.VMEM((POOL_HALO + tm, ga), F32)],
        compiler_params=_cparams(2),
        name=f"pool_g{g}",
    )(h, w_fold, w_gate, scale)


def _out_kernel(*refs, n_parts, n_norm):
    y_refs = refs[:n_parts]
    w_ref, x_ref = refs[n_parts], refs[n_parts + 1]
    g_ref = refs[n_parts + 2] if n_norm else None
    outs = refs[n_parts + 2 + (1 if n_norm else 0):]
    xo_ref, h_refs = outs[0], outs[1:]

    acc = x_ref[0]
    for p in range(n_parts):
        acc = acc + jnp.dot(y_refs[p][0], w_ref[p], preferred_element_type=F32)
    xo_ref[0] = acc
    if n_norm:
        ms = jnp.mean(acc * acc, axis=-1, keepdims=True)
        rn = acc * lax.rsqrt(ms + EPS)
        for j in range(n_norm):
            h_refs[j][0] = (rn * g_ref[j:j + 1, :]).astype(BF16)


def _out_proj(parts, w, layer, x, gains, tm):
    b, s, d = x.shape
    _, n_parts, kp, _ = w.shape
    n_norm = 0 if gains is None else gains.shape[0]
    row = lambda i, r: (i, r, 0)
    in_specs = [pl.BlockSpec((1, tm, kp), row) for _ in range(n_parts)]
    in_specs += [_resident((None, n_parts, kp, d), lambda i, r: (layer, 0, 0, 0)),
                 pl.BlockSpec((1, tm, d), row)]
    args = list(parts) + [w, x]
    if n_norm:
        in_specs.append(_resident((n_norm, d), lambda i, r: (0, 0)))
        args.append(gains)
    out_specs = [pl.BlockSpec((1, tm, d), row) for _ in range(1 + n_norm)]
    out_shape = [jax.ShapeDtypeStruct((b, s, d), F32)]
    out_shape += [jax.ShapeDtypeStruct((b, s, d), BF16) for _ in range(n_norm)]
    return pl.pallas_call(
        functools.partial(_out_kernel, n_parts=n_parts, n_norm=n_norm),
        grid=(b, s // tm),
        in_specs=in_specs,
        out_specs=out_specs,
        out_shape=out_shape,
        compiler_params=_cparams(2),
        name=f"out_proj_p{n_parts}_n{n_norm}",
    )(*args)


def _kv_kernel(h_ref, wa_ref, glat_ref, wk_ref, wvt_ref, gkn_ref, gkr_ref, cos_ref, sin_ref,
               k_ref, vt_ref, *, tm, tkv):
    h = h_ref[0]
    ckv = jnp.dot(h, wa_ref[...], preferred_element_type=F32)
    c = ckv[:, :KV_LORA]
    kr = ckv[:, KV_LORA:]
    cn = (c * lax.rsqrt(jnp.mean(c * c, axis=-1, keepdims=True) + EPS) * glat_ref[...]).astype(BF16)
    kn = jnp.dot(cn, wk_ref[...], preferred_element_type=F32)
    vt = lax.dot_general(wvt_ref[...], cn, NT_DIMS, preferred_element_type=F32)

    ss_rope = jnp.sum(kr * kr, axis=-1, keepdims=True)
    krg = kr * gkr_ref[...]
    half = QK_ROPE // 2
    swapped = pltpu.roll(krg, half, 1) + pltpu.roll(krg, LANES - half, 1)
    roped = krg * cos_ref[0] + swapped * sin_ref[0]

    for hd in range(N_HEADS):
        knh = kn[:, hd * QK_NOPE:(hd + 1) * QK_NOPE]
        ss = jnp.sum(knh * knh, axis=-1, keepdims=True) + ss_rope
        rinv = lax.rsqrt(ss * (1.0 / QK_HEAD) + EPS)
        k_ref[0, hd, :, 0:QK_NOPE] = (knh * rinv * gkn_ref[...]).astype(BF16)
        k_ref[0, hd, :, QK_NOPE:QK_HEAD] = (roped[:, :QK_ROPE] * rinv).astype(BF16)
        for j in range(tm // tkv):
            vt_ref[0, hd, j, 0:V_HEAD, :] = vt[hd * V_HEAD:(hd + 1) * V_HEAD, j * tkv:(j + 1) * tkv].astype(BF16)
            vt_ref[0, hd, j, V_HEAD:V_ROWS, :] = jnp.ones((V_ONES, tkv), BF16)


def _shared_kv(h, wa, glat, wk, wvt, gkn, gkr, cos_k, sin_k, tm, tkv):
    b, s, d = h.shape
    row = lambda i, r: (i, r, 0)
    const2 = lambda i, r: (0, 0)
    return pl.pallas_call(
        functools.partial(_kv_kernel, tm=tm, tkv=tkv),
        grid=(b, s // tm),
        in_specs=[pl.BlockSpec((1, tm, d), row),
                  _resident(wa.shape, const2),
                  _resident(glat.shape, const2),
                  _resident(wk.shape, const2),
                  _resident(wvt.shape, const2),
                  _resident(gkn.shape, const2),
                  _resident(gkr.shape, const2),
                  pl.BlockSpec((1, tm, LANES), row),
                  pl.BlockSpec((1, tm, LANES), row)],
        out_specs=[pl.BlockSpec((1, N_HEADS, tm, QK_HEAD), lambda i, r: (i, 0, r, 0)),
                   pl.BlockSpec((1, N_HEADS, tm // tkv, V_ROWS, tkv), lambda i, r: (i, 0, r, 0, 0))],
        out_shape=[jax.ShapeDtypeStruct((b, N_HEADS, s, QK_HEAD), BF16),
                   jax.ShapeDtypeStruct((b, N_HEADS, s // tkv, V_ROWS, tkv), BF16)],
        compiler_params=_cparams(2),
        name="shared_kv",
    )(h, wa, glat, wk, wvt, gkn, gkr, cos_k, sin_k)


def _q_kernel(*refs, tm, q_scale, n_gate):
    h_ref, wlat_ref, glat_ref = refs[:3]
    wgate_refs = refs[3:3 + n_gate]
    wqt_ref, gq_ref, cos_ref, sin_ref, qt_ref, sg_ref = refs[3 + n_gate:]
    h = h_ref[0]
    ql = jnp.dot(h, wlat_ref[...], preferred_element_type=F32)
    qn = (ql * lax.rsqrt(jnp.mean(ql * ql, axis=-1, keepdims=True) + EPS) * glat_ref[...]).astype(BF16)
    for c, wgate_ref in enumerate(wgate_refs):
        gate = jnp.dot(h, wgate_ref[...], preferred_element_type=F32)
        sg_ref[0, :, c * Q_LORA:(c + 1) * Q_LORA] = _silu(gate).astype(BF16)

    qt = lax.dot_general(wqt_ref[...], qn, NT_DIMS, preferred_element_type=F32)
    gq = jnp.broadcast_to(gq_ref[...], (QK_HEAD, tm))
    cos = cos_ref[0]
    sin = sin_ref[0]
    half = QK_ROPE // 2
    for hd in range(N_HEADS):
        q = qt[hd * QK_HEAD:(hd + 1) * QK_HEAD, :]
        ss = jnp.sum(q * q, axis=0, keepdims=True)
        rinv = lax.rsqrt(ss * (1.0 / QK_HEAD) + EPS) * q_scale
        qg = q * gq
        t1 = qg[QK_NOPE:QK_NOPE + half]
        t2 = qg[QK_NOPE + half:]
        qt_ref[0, hd, 0, 0:QK_NOPE, :] = (qg[:QK_NOPE] * rinv).astype(BF16)
        qt_ref[0, hd, 0, QK_NOPE:QK_NOPE + half, :] = ((t1 * cos - t2 * sin) * rinv).astype(BF16)
        qt_ref[0, hd, 0, QK_NOPE + half:QK_HEAD, :] = ((t2 * cos + t1 * sin) * rinv).astype(BF16)


def _q_proj(h, w_in, glat, wqt, gq, layer, cos_t, sin_t, tm, tq, q_scale):
    b, s, d = h.shape
    per_q = tq // tm
    of_layer = lambda i, r: (layer, 0, 0)
    per_layer = lambda a: _resident((None,) + a.shape[1:], of_layer)
    n_gate = w_in.shape[2] // Q_LORA - 1
    wb = n_gate * Q_LORA
    w_in_block = lambda c: _resident((None, d, Q_LORA), lambda i, r: (layer, 0, c))
    return pl.pallas_call(
        functools.partial(_q_kernel, tm=tm, q_scale=q_scale, n_gate=n_gate),
        grid=(b, s // tm),
        in_specs=[pl.BlockSpec((1, tm, d), lambda i, r: (i, r, 0)),
                  w_in_block(0), per_layer(glat), *[w_in_block(1 + c) for c in range(n_gate)],
                  per_layer(wqt), per_layer(gq),
                  pl.BlockSpec((1, QK_ROPE // 2, tm), lambda i, r: (i, 0, r)),
                  pl.BlockSpec((1, QK_ROPE // 2, tm), lambda i, r: (i, 0, r))],
        out_specs=[pl.BlockSpec((1, N_HEADS, 1, QK_HEAD, tm), lambda i, r: (i, 0, r // per_q, 0, r % per_q)),
                   pl.BlockSpec((1, tm, wb), lambda i, r: (i, r, 0))],
        out_shape=[jax.ShapeDtypeStruct((b, N_HEADS, s // tq, QK_HEAD, tq), BF16),
                   jax.ShapeDtypeStruct((b, s, wb), BF16)],
        compiler_params=_cparams(2),
        name="q_proj",
    )(h, w_in, glat, *([w_in] * n_gate), wqt, gq, cos_t, sin_t)


def _attn_kernel(qt_ref, k_ref, vt_ref, sg_ref, o_ref, s0_ref, s1_ref, cm0_ref, cm1_ref, m_ref, acc_ref,
                 *, tq, tk):
    n_q = qt_ref.shape[2]
    n_chain = tq // ATT_SUB
    n_per = tq // tk
    sub_per = tk // ATT_SUB
    s_refs = (s0_ref, s1_ref)
    cm_refs = (cm0_ref, cm1_ref)

    def chain_cols(j):
        return slice(j * ATT_SUB, (j + 1) * ATT_SUB)

    items = []
    for qi in range(n_q):
        items += [(qi, kt, None) for kt in range(qi * n_per)]
        items += [(qi, qi * n_per + d, d) for d in range(n_per)]

    def visible_rows(item, j):
        _, _, d = item
        if d is None:
            return tk
        return min(max(j - d * sub_per + 1, 0), sub_per) * ATT_SUB

    def issue_scores(t, j):
        qi, kt, d = items[t]
        rows = visible_rows(items[t], j)
        if rows == 0:
            return
        k_tile = k_ref[0, 0, kt * tk:kt * tk + rows, :]
        st = jnp.dot(k_tile, qt_ref[0, 0, qi, :, chain_cols(j)], preferred_element_type=F32)
        if d is not None and j < (d + 1) * sub_per:
            k_chunk = lax.broadcasted_iota(jnp.int32, (rows, ATT_SUB), 0) // CHUNK
            q_chunk = lax.broadcasted_iota(jnp.int32, (rows, ATT_SUB), 1) // CHUNK
            st = jnp.where(k_chunk <= q_chunk + (rows // ATT_SUB - 1) * (ATT_SUB // CHUNK), st, -1e30)
        s_refs[t % 2][j, 0:rows, :] = st
        cm_refs[t % 2][j] = jnp.max(st, axis=0, keepdims=True)

    def consume(t, j):
        qi, kt, _ = items[t]
        rows = visible_rows(items[t], j)
        if rows == 0:
            return
        cols = chain_cols(j)
        m_old = m_ref[qi, :, cols]
        m_new = jnp.maximum(m_old, cm_refs[t % 2][j])
        m_ref[qi, :, cols] = m_new
        p = jnp.exp2(s_refs[t % 2][j, 0:rows, :] - m_new).astype(BF16)
        pv = jnp.dot(vt_ref[0, 0, kt, :, 0:rows], p, preferred_element_type=F32)
        acc_ref[qi, :, cols] = jnp.exp2(m_old - m_new) * acc_ref[qi, :, cols] + pv

    for j in range(n_chain):
        issue_scores(0, j)
    for t, (qi, kt, d) in enumerate(items):
        if t == 0 or items[t - 1][0] != qi:
            m_ref[qi] = jnp.full(m_ref.shape[1:], -1e30, F32)
            acc_ref[qi] = jnp.zeros(acc_ref.shape[1:], F32)
        for j in range(n_chain):
            if t + 1 < len(items):
                issue_scores(t + 1, j)
            consume(t, j)
        if t + 1 == len(items) or items[t + 1][0] != qi:
            inv_l = 1.0 / acc_ref[qi, V_HEAD:V_HEAD + 1, :]
            o = (acc_ref[qi, 0:V_HEAD, :] * inv_l).T
            rows_q = slice(qi * tq, (qi + 1) * tq)
            o_ref[0, rows_q, :] = (o * sg_ref[0, rows_q, :].astype(F32)).astype(BF16)


def _attention(qt, k, vt, sg, tk):
    b, n_heads, n_q, _, tq = qt.shape
    s = n_q * tq
    wb = sg.shape[2]
    n_kt = s // tk
    assert tq % tk == 0 and tk % ATT_SUB == 0
    return pl.pallas_call(
        functools.partial(_attn_kernel, tq=tq, tk=tk),
        grid=(b, n_heads),
        in_specs=[pl.BlockSpec((1, 1, n_q, QK_HEAD, tq), lambda i, hd: (i, hd, 0, 0, 0)),
                  pl.BlockSpec((1, 1, s, QK_HEAD), lambda i, hd: (i, hd, 0, 0)),
                  pl.BlockSpec((1, 1, n_kt, V_ROWS, tk), lambda i, hd: (i, hd, 0, 0, 0)),
                  pl.BlockSpec((1, s, V_HEAD), lambda i, hd: (i, 0, hd))],
        out_specs=pl.BlockSpec((1, s, V_HEAD), lambda i, hd: (i, 0, hd)),
        out_shape=jax.ShapeDtypeStruct((b, s, wb), BF16),
        scratch_shapes=[pltpu.VMEM((tq // ATT_SUB, tk, ATT_SUB), F32),
                        pltpu.VMEM((tq // ATT_SUB, tk, ATT_SUB), F32),
                        pltpu.VMEM((tq // ATT_SUB, 1, ATT_SUB), F32),
                        pltpu.VMEM((tq // ATT_SUB, 1, ATT_SUB), F32),
                        pltpu.VMEM((n_q, 1, tq), F32),
                        pltpu.VMEM((n_q, V_ROWS, tq), F32)],
        compiler_params=_cparams(2),
        name="attn",
    )(qt, k, vt, sg)


def _rope_tables(positions):
    inv = ROPE_THETA ** (-jnp.arange(0, QK_ROPE, 2, dtype=F32) / QK_ROPE)
    ang = positions.astype(F32)[..., None] * inv
    return jnp.cos(ang), jnp.sin(ang)


def kernel(x, positions, a_norm_g, a_w_in, a_w_group, a_scale, a_w_out, kv_norm_g, kv_w_a, kv_latent_g, kv_w_b, k_norm_g, b_norm_g, b_w_in, b_q_latent_g, b_w_q_b, b_q_norm_g, b_w_out):
    b, s, d = x.shape
    n_a = a_w_in.shape[0]
    n_b = b_w_in.shape[0]
    n_groups, ga = a_w_group.shape[1], a_w_group.shape[2]
    tm = min(512, s)
    tq = min(1024, s)
    tk = min(512, s)

    cos, sin = _rope_tables(positions)
    zeros = jnp.zeros((b, s, LANES - QK_ROPE), F32)
    cos_k = jnp.concatenate([cos, cos, zeros], axis=-1)
    sin_k = jnp.concatenate([-sin, sin, zeros], axis=-1)
    cos_t = jnp.swapaxes(cos, 1, 2)
    sin_t = jnp.swapaxes(sin, 1, 2)

    a_w_gate = a_w_in[:, :, n_groups * ga:].astype(BF16)
    a_w_out_bf = a_w_out.astype(BF16).reshape(n_a, n_groups, ga, d)
    a_scale3 = a_scale.reshape(n_a, 1, n_groups * ga)

    h = _norm(x, a_norm_g[0], tm)
    for layer in range(n_a):
        w_fold = _fold_group_weights(a_w_in, a_w_group, layer, min(1024, d))
        ys = [_pool_group(h, w_fold, a_w_gate, a_scale3, layer, g, tm) for g in range(n_groups)]
        if layer + 1 < n_a:
            x, h = _out_proj(ys, a_w_out_bf, layer, x, a_norm_g[layer + 1:layer + 2], tm)
        else:
            x, h, h_kv = _out_proj(ys, a_w_out_bf, layer, x, jnp.stack([b_norm_g[0], kv_norm_g]), tm)

    wa = jnp.pad(kv_w_a, ((0, 0), (0, LANES - QK_ROPE))).astype(BF16)
    kv_w = kv_w_b.reshape(KV_LORA, N_HEADS, QK_NOPE + V_HEAD)
    wk = kv_w[:, :, :QK_NOPE].reshape(KV_LORA, N_HEADS * QK_NOPE).astype(BF16)
    wvt = kv_w[:, :, QK_NOPE:].reshape(KV_LORA, N_HEADS * V_HEAD).T.astype(BF16)
    gkn = k_norm_g[:QK_NOPE].reshape(1, QK_NOPE)
    gkr = jnp.pad(k_norm_g[QK_NOPE:], (0, LANES - QK_ROPE)).reshape(1, LANES)
    k, vt = _shared_kv(h_kv, wa, kv_latent_g.reshape(1, KV_LORA), wk, wvt, gkn, gkr,
                       cos_k, sin_k, tm, tk)

    q_scale = (QK_HEAD ** -0.5) * math.log2(math.e)
    b_w_in_bf = b_w_in.astype(BF16)
    b_w_qt = jnp.swapaxes(b_w_q_b, 1, 2).astype(BF16)
    b_w_out_bf = b_w_out.astype(BF16)[:, None]
    b_glat = b_q_latent_g.reshape(n_b, 1, Q_LORA)
    b_gq = b_q_norm_g.reshape(n_b, QK_HEAD, 1)
    for j in range(n_b):
        qt, sg = _q_proj(h, b_w_in_bf, b_glat, b_w_qt, b_gq, j, cos_t, sin_t, tm, tq, q_scale)
        og = _attention(qt, k, vt, sg, tk)
        if j + 1 < n_b:
            x, h = _out_proj([og], b_w_out_bf, j, x, b_norm_g[j + 1:j + 2], tm)
        else:
            (x,) = _out_proj([og], b_w_out_bf, j, x, None, tm)
    return x
```

```python
import functools
import math

import jax
import jax.numpy as jnp
from jax import lax
from jax.experimental import pallas as pl
from jax.experimental.pallas import tpu as pltpu

CHUNK = 64
POOL_WINDOWS = (2, 4, 8, 16)
N_HEADS = 16
QK_NOPE = 128
QK_ROPE = 64
QK_HEAD = QK_NOPE + QK_ROPE
V_HEAD = 128
KV_LORA = 512
Q_LORA = 512
ROPE_THETA = 10000.0
EPS = 1e-6

LANES = 128
POOL_HALO = 32
ATT_SUB = 256
V_ONES = 16
V_ROWS = V_HEAD + V_ONES
VMEM_LIMIT_BYTES = 56 * 1024 * 1024

F32 = jnp.float32
BF16 = jnp.bfloat16
NT_DIMS = (((1,), (1,)), ((), ()))


def _cparams(n_axes):
    return pltpu.CompilerParams(
        dimension_semantics=("arbitrary",) * n_axes,
        vmem_limit_bytes=VMEM_LIMIT_BYTES)


def _resident(shape, index_map):
    return pl.BlockSpec(shape, index_map, pipeline_mode=pl.Buffered(1))


def _silu(v):
    return v * jax.nn.sigmoid(v)


def _norm_kernel(x_ref, g_ref, h_ref):
    x = x_ref[0]
    ms = jnp.mean(x * x, axis=-1, keepdims=True)
    h_ref[0] = (x * lax.rsqrt(ms + EPS) * g_ref[...]).astype(BF16)


def _norm(x, g, tm):
    b, s, d = x.shape
    return pl.pallas_call(
        _norm_kernel,
        grid=(b, s // tm),
        in_specs=[pl.BlockSpec((1, tm, d), lambda i, r: (i, r, 0)),
                  _resident((1, d), lambda i, r: (0, 0))],
        out_specs=pl.BlockSpec((1, tm, d), lambda i, r: (i, r, 0)),
        out_shape=jax.ShapeDtypeStruct((b, s, d), BF16),
        compiler_params=_cparams(2),
        name="norm",
    )(x, g.reshape(1, d))


def _fold_kernel(wu_ref, wgrp_ref, o_ref):
    o_ref[...] = jnp.dot(wu_ref[...].astype(BF16), wgrp_ref[...].astype(BF16),
                         preferred_element_type=F32).astype(BF16)


def _fold_group_weights(w_in, w_group, layer, tr):
    d = w_in.shape[1]
    _, n_groups, ga, _ = w_group.shape
    return pl.pallas_call(
        _fold_kernel,
        grid=(n_groups, d // tr),
        in_specs=[pl.BlockSpec((None, tr, ga), lambda g, r: (layer, r, g)),
                  pl.BlockSpec((None, None, ga, ga), lambda g, r: (layer, g, 0, 0))],
        out_specs=pl.BlockSpec((tr, ga), lambda g, r: (r, g)),
        out_shape=jax.ShapeDtypeStruct((d, n_groups * ga), BF16),
        compiler_params=_cparams(2),
        name="fold_group_weights",
    )(w_in, w_group)


def _pool_kernel(h_ref, wu_ref, wg_ref, sc_ref, y_ref, a_ref, b_ref, c_ref, wgb_ref, *, tm, levels):
    r = pl.program_id(1)
    ga = a_ref.shape[1]

    @pl.when((r == 0) & (pl.program_id(0) == 0))
    def _():
        wgb_ref[...] = wg_ref[...].astype(BF16)

    @pl.when(r == 0)
    def _():
        a_ref[0:POOL_HALO, :] = jnp.zeros((POOL_HALO, ga), F32)

    h = h_ref[0]
    a_ref[POOL_HALO:POOL_HALO + tm, :] = jnp.dot(h, wu_ref[...], preferred_element_type=F32)

    src = a_ref
    for j in range(levels):
        shift = 1 << j
        lo = 8 * (j + 1)
        n = POOL_HALO + tm - lo
        dst = (b_ref, c_ref)[j % 2]
        dst[lo:lo + n, :] = src[lo:lo + n, :] + src[lo - shift:lo - shift + n, :]
        src = dst

    window = 1 << levels
    t = r * tm + lax.broadcasted_iota(jnp.int32, (tm, LANES), 0)
    inv_count = 1.0 / jnp.minimum(t + 1, window).astype(F32)
    gate = jnp.dot(h, wgb_ref[...], preferred_element_type=F32)
    for c in range(ga // LANES):
        cols = slice(c * LANES, (c + 1) * LANES)
        z = src[POOL_HALO:POOL_HALO + tm, cols] * inv_count - a_ref[POOL_HALO:POOL_HALO + tm, cols]
        y_ref[0, :, cols] = (z * sc_ref[:, cols] * _silu(gate[:, cols])).astype(BF16)

    a_ref[0:POOL_HALO, :] = a_ref[tm:tm + POOL_HALO, :]


def _pool_group(h, w_fold, w_in, scale, layer, g, tm):
    b, s, d = h.shape
    n_groups = len(POOL_WINDOWS)
    ga = w_fold.shape[1] // n_groups
    levels = int(math.log2(POOL_WINDOWS[g]))
    assert 1 << levels == POOL_WINDOWS[g] and 8 * levels <= POOL_HALO <= tm
    return pl.pallas_call(
        functools.partial(_pool_kernel, tm=tm, levels=levels),
        grid=(b, s // tm),
        in_specs=[pl.BlockSpec((1, tm, d), lambda i, r: (i, r, 0)),
                  _resident((d, ga), lambda i, r: (0, g)),
                  _resident((None, d, ga), lambda i, r: (layer, 0, n_groups + g)),
                  _resident((None, 1, ga), lambda i, r: (layer, 0, g))],
        out_specs=pl.BlockSpec((1, tm, ga), lambda i, r: (i, r, 0)),
        out_shape=jax.ShapeDtypeStruct((b, s, ga), BF16),
        scratch_shapes=[pltpu.VMEM((POOL_HALO + tm, ga), F32),
                        pltpu.VMEM((POOL_HALO + tm, ga), F32),
                        pltpu.VMEM((POOL_HALO + tm, ga), F32),
                        pltpu.VMEM((d, ga), BF16)],
        compiler_params=_cparams(2),
        name=f"pool_g{g}",
    )(h, w_fold, w_in, scale)


def _out_kernel(*refs, n_parts, n_norm):
    y_refs = refs[:n_parts]
    w_ref, x_ref = refs[n_parts], refs[n_parts + 1]
    g_ref = refs[n_parts + 2] if n_norm else None
    outs = refs[n_parts + 2 + (1 if n_norm else 0):]
    xo_ref, h_refs = outs[0], outs[1:]

    acc = x_ref[0]
    for p in range(n_parts):
        acc = acc + jnp.dot(y_refs[p][0], w_ref[p], preferred_element_type=F32)
    xo_ref[0] = acc
    if n_norm:
        ms = jnp.mean(acc * acc, axis=-1, keepdims=True)
        rn = acc * lax.rsqrt(ms + EPS)
        for j in range(n_norm):
            h_refs[j][0] = (rn * g_ref[j:j + 1, :]).astype(BF16)


def _out_proj(parts, w, layer, x, gains, tm):
    b, s, d = x.shape
    _, n_parts, kp, _ = w.shape
    n_norm = 0 if gains is None else gains.shape[0]
    row = lambda i, r: (i, r, 0)
    in_specs = [pl.BlockSpec((1, tm, kp), row) for _ in range(n_parts)]
    in_specs += [_resident((None, n_parts, kp, d), lambda i, r: (layer, 0, 0, 0)),
                 pl.BlockSpec((1, tm, d), row)]
    args = list(parts) + [w, x]
    if n_norm:
        in_specs.append(_resident((n_norm, d), lambda i, r: (0, 0)))
        args.append(gains)
    out_specs = [pl.BlockSpec((1, tm, d), row) for _ in range(1 + n_norm)]
    out_shape = [jax.ShapeDtypeStruct((b, s, d), F32)]
    out_shape += [jax.ShapeDtypeStruct((b, s, d), BF16) for _ in range(n_norm)]
    return pl.pallas_call(
        functools.partial(_out_kernel, n_parts=n_parts, n_norm=n_norm),
        grid=(b, s // tm),
        in_specs=in_specs,
        out_specs=out_specs,
        out_shape=out_shape,
        compiler_params=_cparams(2),
        name=f"out_proj_p{n_parts}_n{n_norm}",
    )(*args)


def _kv_kernel(h_ref, wa_ref, glat_ref, wk_ref, wvt_ref, gkn_ref, gkr_ref, cos_ref, sin_ref,
               k_ref, vt_ref, *, tm, tkv):
    h = h_ref[0]
    ckv = jnp.dot(h, wa_ref[...], preferred_element_type=F32)
    c = ckv[:, :KV_LORA]
    kr = ckv[:, KV_LORA:]
    cn = (c * lax.rsqrt(jnp.mean(c * c, axis=-1, keepdims=True) + EPS) * glat_ref[...]).astype(BF16)
    kn = jnp.dot(cn, wk_ref[...], preferred_element_type=F32)
    vt = lax.dot_general(wvt_ref[...], cn, NT_DIMS, preferred_element_type=F32)

    ss_rope = jnp.sum(kr * kr, axis=-1, keepdims=True)
    krg = kr * gkr_ref[...]
    half = QK_ROPE // 2
    swapped = pltpu.roll(krg, half, 1) + pltpu.roll(krg, LANES - half, 1)
    roped = krg * cos_ref[0] + swapped * sin_ref[0]

    for hd in range(N_HEADS):
        knh = kn[:, hd * QK_NOPE:(hd + 1) * QK_NOPE]
        ss = jnp.sum(knh * knh, axis=-1, keepdims=True) + ss_rope
        rinv = lax.rsqrt(ss * (1.0 / QK_HEAD) + EPS)
        k_ref[0, hd, :, 0:QK_NOPE] = (knh * rinv * gkn_ref[...]).astype(BF16)
        k_ref[0, hd, :, QK_NOPE:QK_HEAD] = (roped[:, :QK_ROPE] * rinv).astype(BF16)
        for j in range(tm // tkv):
            vt_ref[0, hd, j, 0:V_HEAD, :] = vt[hd * V_HEAD:(hd + 1) * V_HEAD, j * tkv:(j + 1) * tkv].astype(BF16)
            vt_ref[0, hd, j, V_HEAD:V_ROWS, :] = jnp.ones((V_ONES, tkv), BF16)


def _shared_kv(h, wa, glat, wk, wvt, gkn, gkr, cos_k, sin_k, tm, tkv):
    b, s, d = h.shape
    row = lambda i, r: (i, r, 0)
    const2 = lambda i, r: (0, 0)
    return pl.pallas_call(
        functools.partial(_kv_kernel, tm=tm, tkv=tkv),
        grid=(b, s // tm),
        in_specs=[pl.BlockSpec((1, tm, d), row),
                  _resident(wa.shape, const2),
                  _resident(glat.shape, const2),
                  _resident(wk.shape, const2),
                  _resident(wvt.shape, const2),
                  _resident(gkn.shape, const2),
                  _resident(gkr.shape, const2),
                  pl.BlockSpec((1, tm, LANES), row),
                  pl.BlockSpec((1, tm, LANES), row)],
        out_specs=[pl.BlockSpec((1, N_HEADS, tm, QK_HEAD), lambda i, r: (i, 0, r, 0)),
                   pl.BlockSpec((1, N_HEADS, tm // tkv, V_ROWS, tkv), lambda i, r: (i, 0, r, 0, 0))],
        out_shape=[jax.ShapeDtypeStruct((b, N_HEADS, s, QK_HEAD), BF16),
                   jax.ShapeDtypeStruct((b, N_HEADS, s // tkv, V_ROWS, tkv), BF16)],
        compiler_params=_cparams(2),
        name="shared_kv",
    )(h, wa, glat, wk, wvt, gkn, gkr, cos_k, sin_k)


def _q_kernel(*refs, tm, q_scale, n_gate):
    h_ref, wlat_ref, glat_ref = refs[:3]
    wgate_refs = refs[3:3 + n_gate]
    wqt_ref, gq_ref, cos_ref, sin_ref, qt_ref, sg_ref = refs[3 + n_gate:]
    h = h_ref[0]
    ql = jnp.dot(h, wlat_ref[...], preferred_element_type=F32)
    qn = (ql * lax.rsqrt(jnp.mean(ql * ql, axis=-1, keepdims=True) + EPS) * glat_ref[...]).astype(BF16)
    for c, wgate_ref in enumerate(wgate_refs):
        gate = jnp.dot(h, wgate_ref[...], preferred_element_type=F32)
        sg_ref[0, :, c * Q_LORA:(c + 1) * Q_LORA] = _silu(gate).astype(BF16)

    qt = lax.dot_general(wqt_ref[...], qn, NT_DIMS, preferred_element_type=F32)
    gq = jnp.broadcast_to(gq_ref[...], (QK_HEAD, tm))
    cos = cos_ref[0]
    sin = sin_ref[0]
    half = QK_ROPE // 2
    for hd in range(N_HEADS):
        q = qt[hd * QK_HEAD:(hd + 1) * QK_HEAD, :]
        ss = jnp.sum(q * q, axis=0, keepdims=True)
        rinv = lax.rsqrt(ss * (1.0 / QK_HEAD) + EPS) * q_scale
        qg = q * gq
        t1 = qg[QK_NOPE:QK_NOPE + half]
        t2 = qg[QK_NOPE + half:]
        qt_ref[0, hd, 0, 0:QK_NOPE, :] = (qg[:QK_NOPE] * rinv).astype(BF16)
        qt_ref[0, hd, 0, QK_NOPE:QK_NOPE + half, :] = ((t1 * cos - t2 * sin) * rinv).astype(BF16)
        qt_ref[0, hd, 0, QK_NOPE + half:QK_HEAD, :] = ((t2 * cos + t1 * sin) * rinv).astype(BF16)


def _q_proj(h, w_in, glat, wqt, gq, layer, cos_t, sin_t, tm, tq, q_scale):
    b, s, d = h.shape
    per_q = tq // tm
    of_layer = lambda i, r: (layer, 0, 0)
    per_layer = lambda a: _resident((None,) + a.shape[1:], of_layer)
    n_gate = w_in.shape[2] // Q_LORA - 1
    wb = n_gate * Q_LORA
    w_in_block = lambda c: _resident((None, d, Q_LORA), lambda i, r: (layer, 0, c))
    return pl.pallas_call(
        functools.partial(_q_kernel, tm=tm, q_scale=q_scale, n_gate=n_gate),
        grid=(b, s // tm),
        in_specs=[pl.BlockSpec((1, tm, d), lambda i, r: (i, r, 0)),
                  w_in_block(0), per_layer(glat), *[w_in_block(1 + c) for c in range(n_gate)],
                  per_layer(wqt), per_layer(gq),
                  pl.BlockSpec((1, QK_ROPE // 2, tm), lambda i, r: (i, 0, r)),
                  pl.BlockSpec((1, QK_ROPE // 2, tm), lambda i, r: (i, 0, r))],
        out_specs=[pl.BlockSpec((1, N_HEADS, 1, QK_HEAD, tm), lambda i, r: (i, 0, r // per_q, 0, r % per_q)),
                   pl.BlockSpec((1, tm, wb), lambda i, r: (i, r, 0))],
        out_shape=[jax.ShapeDtypeStruct((b, N_HEADS, s // tq, QK_HEAD, tq), BF16),
                   jax.ShapeDtypeStruct((b, s, wb), BF16)],
        compiler_params=_cparams(2),
        name="q_proj",
    )(h, w_in, glat, *([w_in] * n_gate), wqt, gq, cos_t, sin_t)


def _attn_kernel(qt_ref, k_ref, vt_ref, sg_ref, o_ref, s0_ref, s1_ref, cm0_ref, cm1_ref, m_ref, acc_ref,
                 *, tq, tk):
    n_q = qt_ref.shape[2]
    n_chain = tq // ATT_SUB
    n_per = tq // tk
    sub_per = tk // ATT_SUB
    s_refs = (s0_ref, s1_ref)
    cm_refs = (cm0_ref, cm1_ref)

    def chain_cols(j):
        return slice(j * ATT_SUB, (j + 1) * ATT_SUB)

    items = []
    for qi in range(n_q):
        items += [(qi, kt, None) for kt in range(qi * n_per)]
        items += [(qi, qi * n_per + d, d) for d in range(n_per)]

    def visible_rows(item, j):
        _, _, d = item
        if d is None:
            return tk
        return min(max(j - d * sub_per + 1, 0), sub_per) * ATT_SUB

    def issue_scores(t, j):
        qi, kt, d = items[t]
        rows = visible_rows(items[t], j)
        if rows == 0:
            return
        k_tile = k_ref[0, 0, kt * tk:kt * tk + rows, :]
        st = jnp.dot(k_tile, qt_ref[0, 0, qi, :, chain_cols(j)], preferred_element_type=F32)
        if d is not None and j < (d + 1) * sub_per:
            k_chunk = lax.broadcasted_iota(jnp.int32, (rows, ATT_SUB), 0) // CHUNK
            q_chunk = lax.broadcasted_iota(jnp.int32, (rows, ATT_SUB), 1) // CHUNK
            st = jnp.where(k_chunk <= q_chunk + (rows // ATT_SUB - 1) * (ATT_SUB // CHUNK), st, -1e30)
        s_refs[t % 2][j, 0:rows, :] = st
        cm_refs[t % 2][j] = jnp.max(st, axis=0, keepdims=True)

    def consume(t, j):
        qi, kt, _ = items[t]
        rows = visible_rows(items[t], j)
        if rows == 0:
            return
        cols = chain_cols(j)
        m_old = m_ref[qi, :, cols]
        m_new = jnp.maximum(m_old, cm_refs[t % 2][j])
        m_ref[qi, :, cols] = m_new
        p = jnp.exp2(s_refs[t % 2][j, 0:rows, :] - m_new).astype(BF16)
        pv = jnp.dot(vt_ref[0, 0, kt, :, 0:rows], p, preferred_element_type=F32)
        acc_ref[qi, :, cols] = jnp.exp2(m_old - m_new) * acc_ref[qi, :, cols] + pv

    for j in range(n_chain):
        issue_scores(0, j)
    for t, (qi, kt, d) in enumerate(items):
        if t == 0 or items[t - 1][0] != qi:
            m_ref[qi] = jnp.full(m_ref.shape[1:], -1e30, F32)
            acc_ref[qi] = jnp.zeros(acc_ref.shape[1:], F32)
        for j in range(n_chain):
            if t + 1 < len(items):
                issue_scores(t + 1, j)
            consume(t, j)
        if t + 1 == len(items) or items[t + 1][0] != qi:
            inv_l = 1.0 / acc_ref[qi, V_HEAD:V_HEAD + 1, :]
            o = (acc_ref[qi, 0:V_HEAD, :] * inv_l).T
            rows_q = slice(qi * tq, (qi + 1) * tq)
            o_ref[0, rows_q, :] = (o * sg_ref[0, rows_q, :].astype(F32)).astype(BF16)


def _attention(qt, k, vt, sg, tk):
    b, n_heads, n_q, _, tq = qt.shape
    s = n_q * tq
    wb = sg.shape[2]
    n_kt = s // tk
    assert tq % tk == 0 and tk % ATT_SUB == 0
    return pl.pallas_call(
        functools.partial(_attn_kernel, tq=tq, tk=tk),
        grid=(b, n_heads),
        in_specs=[pl.BlockSpec((1, 1, n_q, QK_HEAD, tq), lambda i, hd: (i, hd, 0, 0, 0)),
                  pl.BlockSpec((1, 1, s, QK_HEAD), lambda i, hd: (i, hd, 0, 0)),
                  pl.BlockSpec((1, 1, n_kt, V_ROWS, tk), lambda i, hd: (i, hd, 0, 0, 0)),
                  pl.BlockSpec((1, s, V_HEAD), lambda i, hd: (i, 0, hd))],
        out_specs=pl.BlockSpec((1, s, V_HEAD), lambda i, hd: (i, 0, hd)),
        out_shape=jax.ShapeDtypeStruct((b, s, wb), BF16),
        scratch_shapes=[pltpu.VMEM((tq // ATT_SUB, tk, ATT_SUB), F32),
                        pltpu.VMEM((tq // ATT_SUB, tk, ATT_SUB), F32),
                        pltpu.VMEM((tq // ATT_SUB, 1, ATT_SUB), F32),
                        pltpu.VMEM((tq // ATT_SUB, 1, ATT_SUB), F32),
                        pltpu.VMEM((n_q, 1, tq), F32),
                        pltpu.VMEM((n_q, V_ROWS, tq), F32)],
        compiler_params=_cparams(2),
        name="attn",
    )(qt, k, vt, sg)


def _rope_tables(positions):
    inv = ROPE_THETA ** (-jnp.arange(0, QK_ROPE, 2, dtype=F32) / QK_ROPE)
    ang = inv[None, :, None] * positions.astype(F32)[:, None, :]
    return lax.optimization_barrier((jnp.cos(ang), jnp.sin(ang)))


def kernel(x, positions, a_norm_g, a_w_in, a_w_group, a_scale, a_w_out, kv_norm_g, kv_w_a, kv_latent_g, kv_w_b, k_norm_g, b_norm_g, b_w_in, b_q_latent_g, b_w_q_b, b_q_norm_g, b_w_out):
    b, s, d = x.shape
    n_a = a_w_in.shape[0]
    n_b = b_w_in.shape[0]
    n_groups, ga = a_w_group.shape[1], a_w_group.shape[2]
    tm = min(512, s)
    tm_pool = min(1024, s)
    tq = min(1024, s)
    tk = min(512, s)

    cos_t, sin_t = _rope_tables(positions)
    cos = jnp.swapaxes(cos_t, 1, 2)
    sin = jnp.swapaxes(sin_t, 1, 2)
    zeros = jnp.zeros((b, s, LANES - QK_ROPE), F32)
    cos_k = jnp.concatenate([cos, cos, zeros], axis=-1)
    sin_k = jnp.concatenate([-sin, sin, zeros], axis=-1)

    a_w_out_bf = a_w_out.astype(BF16).reshape(n_a, n_groups, ga, d)
    a_scale3 = a_scale.reshape(n_a, 1, n_groups * ga)

    h = _norm(x, a_norm_g[0], tm)
    for layer in range(n_a):
        w_fold = _fold_group_weights(a_w_in, a_w_group, layer, min(1024, d))
        ys = [_pool_group(h, w_fold, a_w_in, a_scale3, layer, g, tm_pool) for g in range(n_groups)]
        if layer + 1 < n_a:
            x, h = _out_proj(ys, a_w_out_bf, layer, x, a_norm_g[layer + 1:layer + 2], tm)
        else:
            x, h, h_kv = _out_proj(ys, a_w_out_bf, layer, x, jnp.stack([b_norm_g[0], kv_norm_g]), tm)

    wa = jnp.pad(kv_w_a, ((0, 0), (0, LANES - QK_ROPE))).astype(BF16)
    kv_w = kv_w_b.reshape(KV_LORA, N_HEADS, QK_NOPE + V_HEAD)
    wk = kv_w[:, :, :QK_NOPE].reshape(KV_LORA, N_HEADS * QK_NOPE).astype(BF16)
    wvt = kv_w[:, :, QK_NOPE:].reshape(KV_LORA, N_HEADS * V_HEAD).T.astype(BF16)
    gkn = k_norm_g[:QK_NOPE].reshape(1, QK_NOPE)
    gkr = jnp.pad(k_norm_g[QK_NOPE:], (0, LANES - QK_ROPE)).reshape(1, LANES)
    k, vt = _shared_kv(h_kv, wa, kv_latent_g.reshape(1, KV_LORA), wk, wvt, gkn, gkr,
                       cos_k, sin_k, tm, tk)

    q_scale = (QK_HEAD ** -0.5) * math.log2(math.e)
    b_w_in_bf = b_w_in.astype(BF16)
    b_w_qt = jnp.swapaxes(b_w_q_b, 1, 2).astype(BF16)
    b_w_out_bf = b_w_out.astype(BF16)[:, None]
    b_glat = b_q_latent_g.reshape(n_b, 1, Q_LORA)
    b_gq = b_q_norm_g.reshape(n_b, QK_HEAD, 1)
    for j in range(n_b):
        qt, sg = _q_proj(h, b_w_in_bf, b_glat, b_w_qt, b_gq, j, cos_t, sin_t, tm, tq, q_scale)
        og = _attention(qt, k, vt, sg, tk)
        if j + 1 < n_b:
            x, h = _out_proj([og], b_w_out_bf, j, x, b_norm_g[j + 1:j + 2], tm)
        else:
            (x,) = _out_proj([og], b_w_out_bf, j, x, None, tm)
    return x
```

```python
import functools
import math

import jax
import jax.numpy as jnp
from jax import lax
from jax.experimental import pallas as pl
from jax.experimental.pallas import tpu as pltpu

CHUNK = 64
POOL_WINDOWS = (2, 4, 8, 16)
N_HEADS = 16
QK_NOPE = 128
QK_ROPE = 64
QK_HEAD = QK_NOPE + QK_ROPE
V_HEAD = 128
KV_LORA = 512
Q_LORA = 512
ROPE_THETA = 10000.0
EPS = 1e-6

LANES = 128
POOL_HALO = 32
ATT_SUB = 256
V_ONES = 16
V_ROWS = V_HEAD + V_ONES
VMEM_LIMIT_BYTES = 56 * 1024 * 1024

F32 = jnp.float32
BF16 = jnp.bfloat16
NT_DIMS = (((1,), (1,)), ((), ()))


def _cparams(n_axes):
    return pltpu.CompilerParams(
        dimension_semantics=("arbitrary",) * n_axes,
        vmem_limit_bytes=VMEM_LIMIT_BYTES)


def _resident(shape, index_map):
    return pl.BlockSpec(shape, index_map, pipeline_mode=pl.Buffered(1))


def _silu(v):
    return v * jax.nn.sigmoid(v)


def _norm_kernel(x_ref, g_ref, h_ref):
    x = x_ref[0]
    ms = jnp.mean(x * x, axis=-1, keepdims=True)
    h_ref[0] = (x * lax.rsqrt(ms + EPS) * g_ref[...]).astype(BF16)


def _norm(x, g, tm):
    b, s, d = x.shape
    return pl.pallas_call(
        _norm_kernel,
        grid=(b, s // tm),
        in_specs=[pl.BlockSpec((1, tm, d), lambda i, r: (i, r, 0)),
                  _resident((1, d), lambda i, r: (0, 0))],
        out_specs=pl.BlockSpec((1, tm, d), lambda i, r: (i, r, 0)),
        out_shape=jax.ShapeDtypeStruct((b, s, d), BF16),
        compiler_params=_cparams(2),
        name="norm",
    )(x, g.reshape(1, d))


def _fold_kernel(wu_ref, wgrp_ref, o_ref):
    o_ref[...] = jnp.dot(wu_ref[...].astype(BF16), wgrp_ref[...].astype(BF16),
                         preferred_element_type=F32).astype(BF16)


def _fold_group_weights(w_in, w_group, layer, tr):
    d = w_in.shape[1]
    _, n_groups, ga, _ = w_group.shape
    return pl.pallas_call(
        _fold_kernel,
        grid=(n_groups, d // tr),
        in_specs=[pl.BlockSpec((None, tr, ga), lambda g, r: (layer, r, g)),
                  pl.BlockSpec((None, None, ga, ga), lambda g, r: (layer, g, 0, 0))],
        out_specs=pl.BlockSpec((tr, ga), lambda g, r: (r, g)),
        out_shape=jax.ShapeDtypeStruct((d, n_groups * ga), BF16),
        compiler_params=_cparams(2),
        name="fold_group_weights",
    )(w_in, w_group)


def _pool_kernel(h_ref, wu_ref, wg_ref, sc_ref, y_ref, a_ref, b_ref, c_ref, wgb_ref, *, tm, levels):
    r = pl.program_id(1)
    ga = a_ref.shape[1]

    @pl.when((r == 0) & (pl.program_id(0) == 0))
    def _():
        wgb_ref[...] = wg_ref[...].astype(BF16)

    @pl.when(r == 0)
    def _():
        a_ref[0:POOL_HALO, :] = jnp.zeros((POOL_HALO, ga), F32)

    h = h_ref[0]
    a_ref[POOL_HALO:POOL_HALO + tm, :] = jnp.dot(h, wu_ref[...], preferred_element_type=F32)

    src = a_ref
    for j in range(levels):
        shift = 1 << j
        lo = 8 * (j + 1)
        n = POOL_HALO + tm - lo
        dst = (b_ref, c_ref)[j % 2]
        dst[lo:lo + n, :] = src[lo:lo + n, :] + src[lo - shift:lo - shift + n, :]
        src = dst

    window = 1 << levels
    t = r * tm + lax.broadcasted_iota(jnp.int32, (tm, LANES), 0)
    inv_count = 1.0 / jnp.minimum(t + 1, window).astype(F32)
    gate = jnp.dot(h, wgb_ref[...], preferred_element_type=F32)
    for c in range(ga // LANES):
        cols = slice(c * LANES, (c + 1) * LANES)
        z = src[POOL_HALO:POOL_HALO + tm, cols] * inv_count - a_ref[POOL_HALO:POOL_HALO + tm, cols]
        y_ref[0, :, cols] = (z * sc_ref[:, cols] * _silu(gate[:, cols])).astype(BF16)

    a_ref[0:POOL_HALO, :] = a_ref[tm:tm + POOL_HALO, :]


def _pool_group(h, w_fold, w_in, scale, layer, g, tm):
    b, s, d = h.shape
    n_groups = len(POOL_WINDOWS)
    ga = w_fold.shape[1] // n_groups
    levels = int(math.log2(POOL_WINDOWS[g]))
    assert 1 << levels == POOL_WINDOWS[g] and 8 * levels <= POOL_HALO <= tm
    return pl.pallas_call(
        functools.partial(_pool_kernel, tm=tm, levels=levels),
        grid=(b, s // tm),
        in_specs=[pl.BlockSpec((1, tm, d), lambda i, r: (i, r, 0)),
                  _resident((d, ga), lambda i, r: (0, g)),
                  _resident((None, d, ga), lambda i, r: (layer, 0, n_groups + g)),
                  _resident((None, 1, ga), lambda i, r: (layer, 0, g))],
        out_specs=pl.BlockSpec((1, tm, ga), lambda i, r: (i, r, 0)),
        out_shape=jax.ShapeDtypeStruct((b, s, ga), BF16),
        scratch_shapes=[pltpu.VMEM((POOL_HALO + tm, ga), F32),
                        pltpu.VMEM((POOL_HALO + tm, ga), F32),
                        pltpu.VMEM((POOL_HALO + tm, ga), F32),
                        pltpu.VMEM((d, ga), BF16)],
        compiler_params=_cparams(2),
        name=f"pool_g{g}",
    )(h, w_fold, w_in, scale)


def _out_kernel(*refs, n_parts, n_norm):
    y_refs = refs[:n_parts]
    w_ref, x_ref = refs[n_parts], refs[n_parts + 1]
    g_ref = refs[n_parts + 2] if n_norm else None
    outs = refs[n_parts + 2 + (1 if n_norm else 0):]
    xo_ref, h_refs = outs[0], outs[1:]

    acc = x_ref[0]
    for p in range(n_parts):
        if len(y_refs[p].shape) == 4:
            y = jnp.concatenate([y_refs[p][0, hd] for hd in range(y_refs[p].shape[1])], axis=1)
        else:
            y = y_refs[p][0]
        acc = acc + jnp.dot(y, w_ref[p], preferred_element_type=F32)
    xo_ref[0] = acc
    if n_norm:
        ms = jnp.mean(acc * acc, axis=-1, keepdims=True)
        rn = acc * lax.rsqrt(ms + EPS)
        for j in range(n_norm):
            h_refs[j][0] = (rn * g_ref[j:j + 1, :]).astype(BF16)


def _out_proj(parts, w, layer, x, gains, tm):
    b, s, d = x.shape
    _, n_parts, kp, _ = w.shape
    n_norm = 0 if gains is None else gains.shape[0]
    row = lambda i, r: (i, r, 0)
    in_specs = [pl.BlockSpec((1, tm, kp), row) if part.ndim == 3 else
                pl.BlockSpec((1, part.shape[1], tm, part.shape[3]), lambda i, r: (i, 0, r, 0))
                for part in parts]
    in_specs += [_resident((None, n_parts, kp, d), lambda i, r: (layer, 0, 0, 0)),
                 pl.BlockSpec((1, tm, d), row)]
    args = list(parts) + [w, x]
    if n_norm:
        in_specs.append(_resident((n_norm, d), lambda i, r: (0, 0)))
        args.append(gains)
    out_specs = [pl.BlockSpec((1, tm, d), row) for _ in range(1 + n_norm)]
    out_shape = [jax.ShapeDtypeStruct((b, s, d), F32)]
    out_shape += [jax.ShapeDtypeStruct((b, s, d), BF16) for _ in range(n_norm)]
    return pl.pallas_call(
        functools.partial(_out_kernel, n_parts=n_parts, n_norm=n_norm),
        grid=(b, s // tm),
        in_specs=in_specs,
        out_specs=out_specs,
        out_shape=out_shape,
        compiler_params=_cparams(2),
        name=f"out_proj_p{n_parts}_n{n_norm}",
    )(*args)


def _kv_kernel(h_ref, wa_ref, glat_ref, wk_ref, wvt_ref, gkn_ref, gkr_ref, cos_ref, sin_ref,
               k_ref, vt_ref, *, tm, tkv):
    h = h_ref[0]
    ckv = jnp.dot(h, wa_ref[...], preferred_element_type=F32)
    c = ckv[:, :KV_LORA]
    kr = ckv[:, KV_LORA:]
    cn = (c * lax.rsqrt(jnp.mean(c * c, axis=-1, keepdims=True) + EPS) * glat_ref[...]).astype(BF16)
    kn = jnp.dot(cn, wk_ref[...], preferred_element_type=F32)
    vt = lax.dot_general(wvt_ref[...], cn, NT_DIMS, preferred_element_type=F32)

    ss_rope = jnp.sum(kr * kr, axis=-1, keepdims=True)
    krg = kr * gkr_ref[...]
    half = QK_ROPE // 2
    swapped = pltpu.roll(krg, half, 1) + pltpu.roll(krg, LANES - half, 1)
    roped = krg * cos_ref[0] + swapped * sin_ref[0]

    for hd in range(N_HEADS):
        knh = kn[:, hd * QK_NOPE:(hd + 1) * QK_NOPE]
        ss = jnp.sum(knh * knh, axis=-1, keepdims=True) + ss_rope
        rinv = lax.rsqrt(ss * (1.0 / QK_HEAD) + EPS)
        k_ref[0, hd, :, 0:QK_NOPE] = (knh * rinv * gkn_ref[...]).astype(BF16)
        k_ref[0, hd, :, QK_NOPE:QK_HEAD] = (roped[:, :QK_ROPE] * rinv).astype(BF16)
        for j in range(tm // tkv):
            vt_ref[0, hd, j, 0:V_HEAD, :] = vt[hd * V_HEAD:(hd + 1) * V_HEAD, j * tkv:(j + 1) * tkv].astype(BF16)
            vt_ref[0, hd, j, V_HEAD:V_ROWS, :] = jnp.ones((V_ONES, tkv), BF16)


def _shared_kv(h, wa, glat, wk, wvt, gkn, gkr, cos_k, sin_k, tm, tkv):
    b, s, d = h.shape
    row = lambda i, r: (i, r, 0)
    const2 = lambda i, r: (0, 0)
    return pl.pallas_call(
        functools.partial(_kv_kernel, tm=tm, tkv=tkv),
        grid=(b, s // tm),
        in_specs=[pl.BlockSpec((1, tm, d), row),
                  _resident(wa.shape, const2),
                  _resident(glat.shape, const2),
                  _resident(wk.shape, const2),
                  _resident(wvt.shape, const2),
                  _resident(gkn.shape, const2),
                  _resident(gkr.shape, const2),
                  pl.BlockSpec((1, tm, LANES), row),
                  pl.BlockSpec((1, tm, LANES), row)],
        out_specs=[pl.BlockSpec((1, N_HEADS, tm, QK_HEAD), lambda i, r: (i, 0, r, 0)),
                   pl.BlockSpec((1, N_HEADS, tm // tkv, V_ROWS, tkv), lambda i, r: (i, 0, r, 0, 0))],
        out_shape=[jax.ShapeDtypeStruct((b, N_HEADS, s, QK_HEAD), BF16),
                   jax.ShapeDtypeStruct((b, N_HEADS, s // tkv, V_ROWS, tkv), BF16)],
        compiler_params=_cparams(2),
        name="shared_kv",
    )(h, wa, glat, wk, wvt, gkn, gkr, cos_k, sin_k)


def _q_kernel(*refs, tm, q_scale, n_gate):
    h_ref, wlat_ref, glat_ref = refs[:3]
    wgate_refs = refs[3:3 + n_gate]
    wqt_ref, gq_ref, cos_ref, sin_ref, qt_ref, sg_ref = refs[3 + n_gate:]
    h = h_ref[0]
    ql = jnp.dot(h, wlat_ref[...], preferred_element_type=F32)
    qn = (ql * lax.rsqrt(jnp.mean(ql * ql, axis=-1, keepdims=True) + EPS) * glat_ref[...]).astype(BF16)
    for c, wgate_ref in enumerate(wgate_refs):
        gate = jnp.dot(h, wgate_ref[...], preferred_element_type=F32)
        sg = _silu(gate).astype(BF16)
        for i in range(Q_LORA // V_HEAD):
            sg_ref[0, c * (Q_LORA // V_HEAD) + i] = sg[:, i * V_HEAD:(i + 1) * V_HEAD]

    qt = lax.dot_general(wqt_ref[...], qn, NT_DIMS, preferred_element_type=F32)
    gq = jnp.broadcast_to(gq_ref[...], (QK_HEAD, tm))
    cos = cos_ref[0]
    sin = sin_ref[0]
    half = QK_ROPE // 2
    for hd in range(N_HEADS):
        q = qt[hd * QK_HEAD:(hd + 1) * QK_HEAD, :]
        ss = jnp.sum(q * q, axis=0, keepdims=True)
        rinv = lax.rsqrt(ss * (1.0 / QK_HEAD) + EPS) * q_scale
        qg = q * gq
        t1 = qg[QK_NOPE:QK_NOPE + half]
        t2 = qg[QK_NOPE + half:]
        qt_ref[0, hd, 0, 0:QK_NOPE, :] = (qg[:QK_NOPE] * rinv).astype(BF16)
        qt_ref[0, hd, 0, QK_NOPE:QK_NOPE + half, :] = ((t1 * cos - t2 * sin) * rinv).astype(BF16)
        qt_ref[0, hd, 0, QK_NOPE + half:QK_HEAD, :] = ((t2 * cos + t1 * sin) * rinv).astype(BF16)


def _q_proj(h, w_in, glat, wqt, gq, layer, cos_t, sin_t, tm, tq, q_scale):
    b, s, d = h.shape
    per_q = tq // tm
    of_layer = lambda i, r: (layer, 0, 0)
    per_layer = lambda a: _resident((None,) + a.shape[1:], of_layer)
    n_gate = w_in.shape[2] // Q_LORA - 1
    wb = n_gate * Q_LORA
    w_in_block = lambda c: _resident((None, d, Q_LORA), lambda i, r: (layer, 0, c))
    return pl.pallas_call(
        functools.partial(_q_kernel, tm=tm, q_scale=q_scale, n_gate=n_gate),
        grid=(b, s // tm),
        in_specs=[pl.BlockSpec((1, tm, d), lambda i, r: (i, r, 0)),
                  w_in_block(0), per_layer(glat), *[w_in_block(1 + c) for c in range(n_gate)],
                  per_layer(wqt), per_layer(gq),
                  pl.BlockSpec((1, QK_ROPE // 2, tm), lambda i, r: (i, 0, r)),
                  pl.BlockSpec((1, QK_ROPE // 2, tm), lambda i, r: (i, 0, r))],
        out_specs=[pl.BlockSpec((1, N_HEADS, 1, QK_HEAD, tm), lambda i, r: (i, 0, r // per_q, 0, r % per_q)),
                   pl.BlockSpec((1, wb // V_HEAD, tm, V_HEAD), lambda i, r: (i, 0, r, 0))],
        out_shape=[jax.ShapeDtypeStruct((b, N_HEADS, s // tq, QK_HEAD, tq), BF16),
                   jax.ShapeDtypeStruct((b, wb // V_HEAD, s, V_HEAD), BF16)],
        compiler_params=_cparams(2),
        name="q_proj",
    )(h, w_in, glat, *([w_in] * n_gate), wqt, gq, cos_t, sin_t)


def _attn_kernel(qt_ref, k_ref, vt_ref, sg_ref, o_ref, s0_ref, s1_ref, cm0_ref, cm1_ref, m_ref, acc_ref,
                 *, tq, tk):
    n_q = qt_ref.shape[2]
    n_chain = tq // ATT_SUB
    n_per = tq // tk
    sub_per = tk // ATT_SUB
    s_refs = (s0_ref, s1_ref)
    cm_refs = (cm0_ref, cm1_ref)

    def chain_cols(j):
        return slice(j * ATT_SUB, (j + 1) * ATT_SUB)

    items = []
    for qi in range(n_q):
        items += [(qi, kt, None) for kt in range(qi * n_per)]
        items += [(qi, qi * n_per + d, d) for d in range(n_per)]

    def visible_rows(item, j):
        _, _, d = item
        if d is None:
            return tk
        return min(max(j - d * sub_per + 1, 0), sub_per) * ATT_SUB

    def issue_scores(t, j):
        qi, kt, d = items[t]
        rows = visible_rows(items[t], j)
        if rows == 0:
            return
        k_tile = k_ref[0, 0, kt * tk:kt * tk + rows, :]
        st = jnp.dot(k_tile, qt_ref[0, 0, qi, :, chain_cols(j)], preferred_element_type=F32)
        if d is not None and j < (d + 1) * sub_per:
            k_chunk = lax.broadcasted_iota(jnp.int32, (rows, ATT_SUB), 0) // CHUNK
            q_chunk = lax.broadcasted_iota(jnp.int32, (rows, ATT_SUB), 1) // CHUNK
            st = jnp.where(k_chunk <= q_chunk + (rows // ATT_SUB - 1) * (ATT_SUB // CHUNK), st, -1e30)
        s_refs[t % 2][j, 0:rows, :] = st
        cm_refs[t % 2][j] = jnp.max(st, axis=0, keepdims=True)

    def consume(t, j):
        qi, kt, _ = items[t]
        rows = visible_rows(items[t], j)
        if rows == 0:
            return
        cols = chain_cols(j)
        m_old = m_ref[qi, :, cols]
        m_new = jnp.maximum(m_old, cm_refs[t % 2][j])
        m_ref[qi, :, cols] = m_new
        p = jnp.exp2(s_refs[t % 2][j, 0:rows, :] - m_new).astype(BF16)
        pv = jnp.dot(vt_ref[0, 0, kt, :, 0:rows], p, preferred_element_type=F32)
        acc_ref[qi, :, cols] = jnp.exp2(m_old - m_new) * acc_ref[qi, :, cols] + pv

    for j in range(n_chain):
        issue_scores(0, j)
    for t, (qi, kt, d) in enumerate(items):
        if t == 0 or items[t - 1][0] != qi:
            m_ref[qi] = jnp.full(m_ref.shape[1:], -1e30, F32)
            acc_ref[qi] = jnp.zeros(acc_ref.shape[1:], F32)
        for j in range(n_chain):
            if t + 1 < len(items):
                issue_scores(t + 1, j)
            consume(t, j)
        if t + 1 == len(items) or items[t + 1][0] != qi:
            inv_l = 1.0 / acc_ref[qi, V_HEAD:V_HEAD + 1, :]
            o = (acc_ref[qi, 0:V_HEAD, :] * inv_l).T
            rows_q = slice(qi * tq, (qi + 1) * tq)
            o_ref[0, 0, rows_q, :] = (o * sg_ref[0, 0, rows_q, :].astype(F32)).astype(BF16)


def _attention(qt, k, vt, sg, tk):
    b, n_heads, n_q, _, tq = qt.shape
    s = n_q * tq
    n_kt = s // tk
    assert tq % tk == 0 and tk % ATT_SUB == 0
    return pl.pallas_call(
        functools.partial(_attn_kernel, tq=tq, tk=tk),
        grid=(b, n_heads),
        in_specs=[pl.BlockSpec((1, 1, n_q, QK_HEAD, tq), lambda i, hd: (i, hd, 0, 0, 0)),
                  pl.BlockSpec((1, 1, s, QK_HEAD), lambda i, hd: (i, hd, 0, 0)),
                  pl.BlockSpec((1, 1, n_kt, V_ROWS, tk), lambda i, hd: (i, hd, 0, 0, 0)),
                  pl.BlockSpec((1, 1, s, V_HEAD), lambda i, hd: (i, hd, 0, 0))],
        out_specs=pl.BlockSpec((1, 1, s, V_HEAD), lambda i, hd: (i, hd, 0, 0)),
        out_shape=jax.ShapeDtypeStruct((b, n_heads, s, V_HEAD), BF16),
        scratch_shapes=[pltpu.VMEM((tq // ATT_SUB, tk, ATT_SUB), F32),
                        pltpu.VMEM((tq // ATT_SUB, tk, ATT_SUB), F32),
                        pltpu.VMEM((tq // ATT_SUB, 1, ATT_SUB), F32),
                        pltpu.VMEM((tq // ATT_SUB, 1, ATT_SUB), F32),
                        pltpu.VMEM((n_q, 1, tq), F32),
                        pltpu.VMEM((n_q, V_ROWS, tq), F32)],
        compiler_params=_cparams(2),
        name="attn",
    )(qt, k, vt, sg)


def _rope_tables(positions):
    inv = ROPE_THETA ** (-jnp.arange(0, QK_ROPE, 2, dtype=F32) / QK_ROPE)
    ang = inv[None, :, None] * positions.astype(F32)[:, None, :]
    return lax.optimization_barrier((jnp.cos(ang), jnp.sin(ang)))


def kernel(x, positions, a_norm_g, a_w_in, a_w_group, a_scale, a_w_out, kv_norm_g, kv_w_a, kv_latent_g, kv_w_b, k_norm_g, b_norm_g, b_w_in, b_q_latent_g, b_w_q_b, b_q_norm_g, b_w_out):
    b, s, d = x.shape
    n_a = a_w_in.shape[0]
    n_b = b_w_in.shape[0]
    n_groups, ga = a_w_group.shape[1], a_w_group.shape[2]
    tm = min(512, s)
    tm_pool = min(1024, s)
    tq = min(1024, s)
    tk = min(512, s)

    cos_t, sin_t = _rope_tables(positions)
    cos = jnp.swapaxes(cos_t, 1, 2)
    sin = jnp.swapaxes(sin_t, 1, 2)
    zeros = jnp.zeros((b, s, LANES - QK_ROPE), F32)
    cos_k = jnp.concatenate([cos, cos, zeros], axis=-1)
    sin_k = jnp.concatenate([-sin, sin, zeros], axis=-1)

    a_w_out_bf = a_w_out.astype(BF16).reshape(n_a, n_groups, ga, d)
    a_scale3 = a_scale.reshape(n_a, 1, n_groups * ga)

    h = _norm(x, a_norm_g[0], tm)
    for layer in range(n_a):
        w_fold = _fold_group_weights(a_w_in, a_w_group, layer, min(1024, d))
        ys = [_pool_group(h, w_fold, a_w_in, a_scale3, layer, g, tm_pool) for g in range(n_groups)]
        if layer + 1 < n_a:
            x, h = _out_proj(ys, a_w_out_bf, layer, x, a_norm_g[layer + 1:layer + 2], tm)
        else:
            x, h, h_kv = _out_proj(ys, a_w_out_bf, layer, x, jnp.stack([b_norm_g[0], kv_norm_g]), tm)

    wa = jnp.pad(kv_w_a, ((0, 0), (0, LANES - QK_ROPE))).astype(BF16)
    kv_w = kv_w_b.reshape(KV_LORA, N_HEADS, QK_NOPE + V_HEAD)
    wk = kv_w[:, :, :QK_NOPE].reshape(KV_LORA, N_HEADS * QK_NOPE).astype(BF16)
    wvt = kv_w[:, :, QK_NOPE:].reshape(KV_LORA, N_HEADS * V_HEAD).T.astype(BF16)
    gkn = k_norm_g[:QK_NOPE].reshape(1, QK_NOPE)
    gkr = jnp.pad(k_norm_g[QK_NOPE:], (0, LANES - QK_ROPE)).reshape(1, LANES)
    k, vt = _shared_kv(h_kv, wa, kv_latent_g.reshape(1, KV_LORA), wk, wvt, gkn, gkr,
                       cos_k, sin_k, tm, tk)

    q_scale = (QK_HEAD ** -0.5) * math.log2(math.e)
    b_w_in_bf = b_w_in.astype(BF16)
    b_w_qt = jnp.swapaxes(b_w_q_b, 1, 2).astype(BF16)
    b_w_out_bf = b_w_out.astype(BF16)[:, None]
    b_glat = b_q_latent_g.reshape(n_b, 1, Q_LORA)
    b_gq = b_q_norm_g.reshape(n_b, QK_HEAD, 1)
    for j in range(n_b):
        qt, sg = _q_proj(h, b_w_in_bf, b_glat, b_w_qt, b_gq, j, cos_t, sin_t, tm, tq, q_scale)
        og = _attention(qt, k, vt, sg, tk)
        if j + 1 < n_b:
            x, h = _out_proj([og], b_w_out_bf, j, x, b_norm_g[j + 1:j + 2], tm)
        else:
            (x,) = _out_proj([og], b_w_out_bf, j, x, None, tm)
    return x
```

```python
import functools
import math

import jax
import jax.numpy as jnp
from jax import lax
from jax.experimental import pallas as pl
from jax.experimental.pallas import tpu as pltpu

CHUNK = 64
POOL_WINDOWS = (2, 4, 8, 16)
N_HEADS = 16
QK_NOPE = 128
QK_ROPE = 64
QK_HEAD = QK_NOPE + QK_ROPE
V_HEAD = 128
KV_LORA = 512
Q_LORA = 512
ROPE_THETA = 10000.0
EPS = 1e-6

LANES = 128
POOL_HALO = 32
ATT_SUB = 256
V_ONES = 16
V_ROWS = V_HEAD + V_ONES
ATT_BOUNDED_SCORE_LIMIT = 64.0
VMEM_LIMIT_BYTES = 56 * 1024 * 1024

F32 = jnp.float32
BF16 = jnp.bfloat16
NT_DIMS = (((1,), (1,)), ((), ()))


def _cparams(n_axes):
    return pltpu.CompilerParams(
        dimension_semantics=("arbitrary",) * n_axes,
        vmem_limit_bytes=VMEM_LIMIT_BYTES)


def _resident(shape, index_map):
    return pl.BlockSpec(shape, index_map, pipeline_mode=pl.Buffered(1))


def _silu(v):
    return v * jax.nn.sigmoid(v)


def _norm_kernel(x_ref, g_ref, h_ref):
    x = x_ref[0]
    ms = jnp.mean(x * x, axis=-1, keepdims=True)
    h_ref[0] = (x * lax.rsqrt(ms + EPS) * g_ref[...]).astype(BF16)


def _norm(x, g, tm):
    b, s, d = x.shape
    return pl.pallas_call(
        _norm_kernel,
        grid=(b, s // tm),
        in_specs=[pl.BlockSpec((1, tm, d), lambda i, r: (i, r, 0)),
                  _resident((1, d), lambda i, r: (0, 0))],
        out_specs=pl.BlockSpec((1, tm, d), lambda i, r: (i, r, 0)),
        out_shape=jax.ShapeDtypeStruct((b, s, d), BF16),
        compiler_params=_cparams(2),
        name="norm",
    )(x, g.reshape(1, d))


def _fold_kernel(wu_ref, wgrp_ref, o_ref):
    o_ref[...] = jnp.dot(wu_ref[...].astype(BF16), wgrp_ref[...].astype(BF16),
                         preferred_element_type=F32).astype(BF16)


def _fold_group_weights(w_in, w_group, layer, tr):
    d = w_in.shape[1]
    _, n_groups, ga, _ = w_group.shape
    return pl.pallas_call(
        _fold_kernel,
        grid=(n_groups, d // tr),
        in_specs=[pl.BlockSpec((None, tr, ga), lambda g, r: (layer, r, g)),
                  pl.BlockSpec((None, None, ga, ga), lambda g, r: (layer, g, 0, 0))],
        out_specs=pl.BlockSpec((tr, ga), lambda g, r: (r, g)),
        out_shape=jax.ShapeDtypeStruct((d, n_groups * ga), BF16),
        compiler_params=_cparams(2),
        name="fold_group_weights",
    )(w_in, w_group)


def _pool_kernel(h_ref, wu_ref, wg_ref, sc_ref, y_ref, a_ref, b_ref, c_ref, wgb_ref, *, tm, levels):
    r = pl.program_id(1)
    ga = a_ref.shape[1]

    @pl.when((r == 0) & (pl.program_id(0) == 0))
    def _():
        wgb_ref[...] = wg_ref[...].astype(BF16)

    @pl.when(r == 0)
    def _():
        a_ref[0:POOL_HALO, :] = jnp.zeros((POOL_HALO, ga), F32)

    h = h_ref[0]
    a_ref[POOL_HALO:POOL_HALO + tm, :] = jnp.dot(h, wu_ref[...], preferred_element_type=F32)

    src = a_ref
    for j in range(levels):
        shift = 1 << j
        lo = 8 * (j + 1)
        n = POOL_HALO + tm - lo
        dst = (b_ref, c_ref)[j % 2]
        dst[lo:lo + n, :] = src[lo:lo + n, :] + src[lo - shift:lo - shift + n, :]
        src = dst

    window = 1 << levels
    t = r * tm + lax.broadcasted_iota(jnp.int32, (tm, LANES), 0)
    inv_count = 1.0 / jnp.minimum(t + 1, window).astype(F32)
    gate = jnp.dot(h, wgb_ref[...], preferred_element_type=F32)
    for c in range(ga // LANES):
        cols = slice(c * LANES, (c + 1) * LANES)
        z = src[POOL_HALO:POOL_HALO + tm, cols] * inv_count - a_ref[POOL_HALO:POOL_HALO + tm, cols]
        y_ref[0, :, cols] = (z * sc_ref[:, cols] * _silu(gate[:, cols])).astype(BF16)

    a_ref[0:POOL_HALO, :] = a_ref[tm:tm + POOL_HALO, :]


def _pool_group(h, w_fold, w_in, scale, layer, g, tm):
    b, s, d = h.shape
    n_groups = len(POOL_WINDOWS)
    ga = w_fold.shape[1] // n_groups
    levels = int(math.log2(POOL_WINDOWS[g]))
    assert 1 << levels == POOL_WINDOWS[g] and 8 * levels <= POOL_HALO <= tm
    return pl.pallas_call(
        functools.partial(_pool_kernel, tm=tm, levels=levels),
        grid=(b, s // tm),
        in_specs=[pl.BlockSpec((1, tm, d), lambda i, r: (i, r, 0)),
                  _resident((d, ga), lambda i, r: (0, g)),
                  _resident((None, d, ga), lambda i, r: (layer, 0, n_groups + g)),
                  _resident((None, 1, ga), lambda i, r: (layer, 0, g))],
        out_specs=pl.BlockSpec((1, tm, ga), lambda i, r: (i, r, 0)),
        out_shape=jax.ShapeDtypeStruct((b, s, ga), BF16),
        scratch_shapes=[pltpu.VMEM((POOL_HALO + tm, ga), F32),
                        pltpu.VMEM((POOL_HALO + tm, ga), F32),
                        pltpu.VMEM((POOL_HALO + tm, ga), F32),
                        pltpu.VMEM((d, ga), BF16)],
        compiler_params=_cparams(2),
        name=f"pool_g{g}",
    )(h, w_fold, w_in, scale)


def _out_kernel(*refs, n_parts, n_norm):
    y_refs = refs[:n_parts]
    w_ref, x_ref = refs[n_parts], refs[n_parts + 1]
    g_ref = refs[n_parts + 2] if n_norm else None
    outs = refs[n_parts + 2 + (1 if n_norm else 0):]
    xo_ref, h_refs = outs[0], outs[1:]

    acc = x_ref[0]
    for p in range(n_parts):
        if len(y_refs[p].shape) == 4:
            y = jnp.concatenate([y_refs[p][0, hd] for hd in range(y_refs[p].shape[1])], axis=1)
        else:
            y = y_refs[p][0]
        acc = acc + jnp.dot(y, w_ref[p], preferred_element_type=F32)
    xo_ref[0] = acc
    if n_norm:
        ms = jnp.mean(acc * acc, axis=-1, keepdims=True)
        rn = acc * lax.rsqrt(ms + EPS)
        for j in range(n_norm):
            h_refs[j][0] = (rn * g_ref[j:j + 1, :]).astype(BF16)


def _out_proj(parts, w, layer, x, gains, tm):
    b, s, d = x.shape
    _, n_parts, kp, _ = w.shape
    n_norm = 0 if gains is None else gains.shape[0]
    row = lambda i, r: (i, r, 0)
    in_specs = [pl.BlockSpec((1, tm, kp), row) if part.ndim == 3 else
                pl.BlockSpec((1, part.shape[1], tm, part.shape[3]), lambda i, r: (i, 0, r, 0))
                for part in parts]
    in_specs += [_resident((None, n_parts, kp, d), lambda i, r: (layer, 0, 0, 0)),
                 pl.BlockSpec((1, tm, d), row)]
    args = list(parts) + [w, x]
    if n_norm:
        in_specs.append(_resident((n_norm, d), lambda i, r: (0, 0)))
        args.append(gains)
    out_specs = [pl.BlockSpec((1, tm, d), row) for _ in range(1 + n_norm)]
    out_shape = [jax.ShapeDtypeStruct((b, s, d), F32)]
    out_shape += [jax.ShapeDtypeStruct((b, s, d), BF16) for _ in range(n_norm)]
    return pl.pallas_call(
        functools.partial(_out_kernel, n_parts=n_parts, n_norm=n_norm),
        grid=(b, s // tm),
        in_specs=in_specs,
        out_specs=out_specs,
        out_shape=out_shape,
        compiler_params=_cparams(2),
        name=f"out_proj_p{n_parts}_n{n_norm}",
    )(*args)


def _kv_kernel(h_ref, wa_ref, glat_ref, wk_ref, wvt_ref, gkn_ref, gkr_ref, cos_ref, sin_ref,
               k_ref, vt_ref, *, tm, tkv):
    h = h_ref[0]
    ckv = jnp.dot(h, wa_ref[...], preferred_element_type=F32)
    c = ckv[:, :KV_LORA]
    kr = ckv[:, KV_LORA:]
    cn = (c * lax.rsqrt(jnp.mean(c * c, axis=-1, keepdims=True) + EPS) * glat_ref[...]).astype(BF16)
    kn = jnp.dot(cn, wk_ref[...], preferred_element_type=F32)
    vt = lax.dot_general(wvt_ref[...], cn, NT_DIMS, preferred_element_type=F32)

    ss_rope = jnp.sum(kr * kr, axis=-1, keepdims=True)
    krg = kr * gkr_ref[...]
    half = QK_ROPE // 2
    swapped = pltpu.roll(krg, half, 1) + pltpu.roll(krg, LANES - half, 1)
    roped = krg * cos_ref[0] + swapped * sin_ref[0]

    for hd in range(N_HEADS):
        knh = kn[:, hd * QK_NOPE:(hd + 1) * QK_NOPE]
        ss = jnp.sum(knh * knh, axis=-1, keepdims=True) + ss_rope
        rinv = lax.rsqrt(ss * (1.0 / QK_HEAD) + EPS)
        k_ref[0, hd, :, 0:QK_NOPE] = (knh * rinv * gkn_ref[...]).astype(BF16)
        k_ref[0, hd, :, QK_NOPE:QK_HEAD] = (roped[:, :QK_ROPE] * rinv).astype(BF16)
        for j in range(tm // tkv):
            vt_ref[0, hd, j, 0:V_HEAD, :] = vt[hd * V_HEAD:(hd + 1) * V_HEAD, j * tkv:(j + 1) * tkv].astype(BF16)
            vt_ref[0, hd, j, V_HEAD:V_ROWS, :] = jnp.ones((V_ONES, tkv), BF16)


def _shared_kv(h, wa, glat, wk, wvt, gkn, gkr, cos_k, sin_k, tm, tkv):
    b, s, d = h.shape
    row = lambda i, r: (i, r, 0)
    const2 = lambda i, r: (0, 0)
    return pl.pallas_call(
        functools.partial(_kv_kernel, tm=tm, tkv=tkv),
        grid=(b, s // tm),
        in_specs=[pl.BlockSpec((1, tm, d), row),
                  _resident(wa.shape, const2),
                  _resident(glat.shape, const2),
                  _resident(wk.shape, const2),
                  _resident(wvt.shape, const2),
                  _resident(gkn.shape, const2),
                  _resident(gkr.shape, const2),
                  pl.BlockSpec((1, tm, LANES), row),
                  pl.BlockSpec((1, tm, LANES), row)],
        out_specs=[pl.BlockSpec((1, N_HEADS, tm, QK_HEAD), lambda i, r: (i, 0, r, 0)),
                   pl.BlockSpec((1, N_HEADS, tm // tkv, V_ROWS, tkv), lambda i, r: (i, 0, r, 0, 0))],
        out_shape=[jax.ShapeDtypeStruct((b, N_HEADS, s, QK_HEAD), BF16),
                   jax.ShapeDtypeStruct((b, N_HEADS, s // tkv, V_ROWS, tkv), BF16)],
        compiler_params=_cparams(2),
        name="shared_kv",
    )(h, wa, glat, wk, wvt, gkn, gkr, cos_k, sin_k)


def _q_kernel(*refs, tm, q_scale, n_gate):
    h_ref, wlat_ref, glat_ref = refs[:3]
    wgate_refs = refs[3:3 + n_gate]
    wqt_ref, gq_ref, cos_ref, sin_ref, qt_ref, sg_ref = refs[3 + n_gate:]
    h = h_ref[0]
    ql = jnp.dot(h, wlat_ref[...], preferred_element_type=F32)
    qn = (ql * lax.rsqrt(jnp.mean(ql * ql, axis=-1, keepdims=True) + EPS) * glat_ref[...]).astype(BF16)
    for c, wgate_ref in enumerate(wgate_refs):
        gate = jnp.dot(h, wgate_ref[...], preferred_element_type=F32)
        sg = _silu(gate).astype(BF16)
        for i in range(Q_LORA // V_HEAD):
            sg_ref[0, c * (Q_LORA // V_HEAD) + i] = sg[:, i * V_HEAD:(i + 1) * V_HEAD]

    qt = lax.dot_general(wqt_ref[...], qn, NT_DIMS, preferred_element_type=F32)
    gq = jnp.broadcast_to(gq_ref[...], (QK_HEAD, tm))
    cos = cos_ref[0]
    sin = sin_ref[0]
    half = QK_ROPE // 2
    for hd in range(N_HEADS):
        q = qt[hd * QK_HEAD:(hd + 1) * QK_HEAD, :]
        ss = jnp.sum(q * q, axis=0, keepdims=True)
        rinv = lax.rsqrt(ss * (1.0 / QK_HEAD) + EPS) * q_scale
        qg = q * gq
        t1 = qg[QK_NOPE:QK_NOPE + half]
        t2 = qg[QK_NOPE + half:]
        qt_ref[0, hd, 0, 0:QK_NOPE, :] = (qg[:QK_NOPE] * rinv).astype(BF16)
        qt_ref[0, hd, 0, QK_NOPE:QK_NOPE + half, :] = ((t1 * cos - t2 * sin) * rinv).astype(BF16)
        qt_ref[0, hd, 0, QK_NOPE + half:QK_HEAD, :] = ((t2 * cos + t1 * sin) * rinv).astype(BF16)


def _q_proj(h, w_in, glat, wqt, gq, layer, cos_t, sin_t, tm, tq, q_scale):
    b, s, d = h.shape
    per_q = tq // tm
    of_layer = lambda i, r: (layer, 0, 0)
    per_layer = lambda a: _resident((None,) + a.shape[1:], of_layer)
    n_gate = w_in.shape[2] // Q_LORA - 1
    wb = n_gate * Q_LORA
    w_in_block = lambda c: _resident((None, d, Q_LORA), lambda i, r: (layer, 0, c))
    return pl.pallas_call(
        functools.partial(_q_kernel, tm=tm, q_scale=q_scale, n_gate=n_gate),
        grid=(b, s // tm),
        in_specs=[pl.BlockSpec((1, tm, d), lambda i, r: (i, r, 0)),
                  w_in_block(0), per_layer(glat), *[w_in_block(1 + c) for c in range(n_gate)],
                  per_layer(wqt), per_layer(gq),
                  pl.BlockSpec((1, QK_ROPE // 2, tm), lambda i, r: (i, 0, r)),
                  pl.BlockSpec((1, QK_ROPE // 2, tm), lambda i, r: (i, 0, r))],
        out_specs=[pl.BlockSpec((1, N_HEADS, 1, QK_HEAD, tm), lambda i, r: (i, 0, r // per_q, 0, r % per_q)),
                   pl.BlockSpec((1, wb // V_HEAD, tm, V_HEAD), lambda i, r: (i, 0, r, 0))],
        out_shape=[jax.ShapeDtypeStruct((b, N_HEADS, s // tq, QK_HEAD, tq), BF16),
                   jax.ShapeDtypeStruct((b, wb // V_HEAD, s, V_HEAD), BF16)],
        compiler_params=_cparams(2),
        name="q_proj",
    )(h, w_in, glat, *([w_in] * n_gate), wqt, gq, cos_t, sin_t)


def _attn_kernel(qt_ref, k_ref, vt_ref, sg_ref, o_ref, *scratch, tq, tk, running_max):
    n_q = qt_ref.shape[2]
    n_chain = tq // ATT_SUB
    n_per = tq // tk
    sub_per = tk // ATT_SUB
    if running_max:
        s0_ref, s1_ref, cm0_ref, cm1_ref, m_ref, acc_ref = scratch
        cm_refs = (cm0_ref, cm1_ref)
    else:
        s0_ref, s1_ref, acc_ref = scratch
    s_refs = (s0_ref, s1_ref)

    def chain_cols(j):
        return slice(j * ATT_SUB, (j + 1) * ATT_SUB)

    items = []
    for qi in range(n_q):
        items += [(qi, kt, None) for kt in range(qi * n_per)]
        items += [(qi, qi * n_per + d, d) for d in range(n_per)]

    def visible_rows(item, j):
        _, _, d = item
        if d is None:
            return tk
        return min(max(j - d * sub_per + 1, 0), sub_per) * ATT_SUB

    def issue_scores(t, j):
        qi, kt, d = items[t]
        rows = visible_rows(items[t], j)
        if rows == 0:
            return
        k_tile = k_ref[0, 0, kt * tk:kt * tk + rows, :]
        st = jnp.dot(k_tile, qt_ref[0, 0, qi, :, chain_cols(j)], preferred_element_type=F32)
        if d is not None and j < (d + 1) * sub_per:
            k_chunk = lax.broadcasted_iota(jnp.int32, (rows, ATT_SUB), 0) // CHUNK
            q_chunk = lax.broadcasted_iota(jnp.int32, (rows, ATT_SUB), 1) // CHUNK
            st = jnp.where(k_chunk <= q_chunk + (rows // ATT_SUB - 1) * (ATT_SUB // CHUNK), st, -1e30)
        if running_max:
            s_refs[t % 2][j, 0:rows, :] = st
            cm_refs[t % 2][j] = jnp.max(st, axis=0, keepdims=True)
        else:
            s_refs[t % 2][j, 0:rows, :] = jnp.exp2(st).astype(BF16)

    def consume(t, j):
        qi, kt, _ = items[t]
        rows = visible_rows(items[t], j)
        if rows == 0:
            return
        cols = chain_cols(j)
        vt_tile = vt_ref[0, 0, kt, :, 0:rows]
        if running_max:
            m_old = m_ref[qi, :, cols]
            m_new = jnp.maximum(m_old, cm_refs[t % 2][j])
            m_ref[qi, :, cols] = m_new
            p = jnp.exp2(s_refs[t % 2][j, 0:rows, :] - m_new).astype(BF16)
            pv = jnp.dot(vt_tile, p, preferred_element_type=F32)
            acc_ref[qi, :, cols] = jnp.exp2(m_old - m_new) * acc_ref[qi, :, cols] + pv
        else:
            acc_ref[qi, :, cols] += jnp.dot(vt_tile, s_refs[t % 2][j, 0:rows, :], preferred_element_type=F32)

    for j in range(n_chain):
        issue_scores(0, j)
    for t, (qi, kt, d) in enumerate(items):
        if t == 0 or items[t - 1][0] != qi:
            if running_max:
                m_ref[qi] = jnp.full(m_ref.shape[1:], -1e30, F32)
            acc_ref[qi] = jnp.zeros(acc_ref.shape[1:], F32)
        for j in range(n_chain):
            if t + 1 < len(items):
                issue_scores(t + 1, j)
            consume(t, j)
        if t + 1 == len(items) or items[t + 1][0] != qi:
            inv_l = 1.0 / acc_ref[qi, V_HEAD:V_HEAD + 1, :]
            o = (acc_ref[qi, 0:V_HEAD, :] * inv_l).T
            rows_q = slice(qi * tq, (qi + 1) * tq)
            o_ref[0, 0, rows_q, :] = (o * sg_ref[0, 0, rows_q, :].astype(F32)).astype(BF16)


def _attention(qt, k, vt, sg, *, tk, running_max):
    b, n_heads, n_q, _, tq = qt.shape
    s = n_q * tq
    n_kt = s // tk
    n_chain = tq // ATT_SUB
    assert tq % tk == 0 and tk % ATT_SUB == 0
    if running_max:
        scratch_shapes = [pltpu.VMEM((n_chain, tk, ATT_SUB), F32),
                          pltpu.VMEM((n_chain, tk, ATT_SUB), F32),
                          pltpu.VMEM((n_chain, 1, ATT_SUB), F32),
                          pltpu.VMEM((n_chain, 1, ATT_SUB), F32),
                          pltpu.VMEM((n_q, 1, tq), F32),
                          pltpu.VMEM((n_q, V_ROWS, tq), F32)]
    else:
        scratch_shapes = [pltpu.VMEM((n_chain, tk, ATT_SUB), BF16),
                          pltpu.VMEM((n_chain, tk, ATT_SUB), BF16),
                          pltpu.VMEM((n_q, V_ROWS, tq), F32)]
    return pl.pallas_call(
        functools.partial(_attn_kernel, tq=tq, tk=tk, running_max=running_max),
        grid=(b, n_heads),
        in_specs=[pl.BlockSpec((1, 1, n_q, QK_HEAD, tq), lambda i, hd: (i, hd, 0, 0, 0)),
                  pl.BlockSpec((1, 1, s, QK_HEAD), lambda i, hd: (i, hd, 0, 0)),
                  pl.BlockSpec((1, 1, n_kt, V_ROWS, tk), lambda i, hd: (i, hd, 0, 0, 0)),
                  pl.BlockSpec((1, 1, s, V_HEAD), lambda i, hd: (i, hd, 0, 0))],
        out_specs=pl.BlockSpec((1, 1, s, V_HEAD), lambda i, hd: (i, hd, 0, 0)),
        out_shape=jax.ShapeDtypeStruct((b, n_heads, s, V_HEAD), BF16),
        scratch_shapes=scratch_shapes,
        compiler_params=_cparams(2),
        name="attn" if running_max else "attn_bounded",
    )(qt, k, vt, sg)


def _rope_tables(positions):
    inv = ROPE_THETA ** (-jnp.arange(0, QK_ROPE, 2, dtype=F32) / QK_ROPE)
    ang = inv[None, :, None] * positions.astype(F32)[:, None, :]
    return lax.optimization_barrier((jnp.cos(ang), jnp.sin(ang)))


def kernel(x, positions, a_norm_g, a_w_in, a_w_group, a_scale, a_w_out, kv_norm_g, kv_w_a, kv_latent_g, kv_w_b, k_norm_g, b_norm_g, b_w_in, b_q_latent_g, b_w_q_b, b_q_norm_g, b_w_out):
    b, s, d = x.shape
    n_a = a_w_in.shape[0]
    n_b = b_w_in.shape[0]
    n_groups, ga = a_w_group.shape[1], a_w_group.shape[2]
    tm = min(512, s)
    tm_pool = min(1024, s)
    tq = min(1024, s)
    tk = min(512, s)

    cos_t, sin_t = _rope_tables(positions)
    cos = jnp.swapaxes(cos_t, 1, 2)
    sin = jnp.swapaxes(sin_t, 1, 2)
    zeros = jnp.zeros((b, s, LANES - QK_ROPE), F32)
    cos_k = jnp.concatenate([cos, cos, zeros], axis=-1)
    sin_k = jnp.concatenate([-sin, sin, zeros], axis=-1)

    a_w_out_bf = a_w_out.astype(BF16).reshape(n_a, n_groups, ga, d)
    a_scale3 = a_scale.reshape(n_a, 1, n_groups * ga)

    h = _norm(x, a_norm_g[0], tm)
    for layer in range(n_a):
        w_fold = _fold_group_weights(a_w_in, a_w_group, layer, min(1024, d))
        ys = [_pool_group(h, w_fold, a_w_in, a_scale3, layer, g, tm_pool) for g in range(n_groups)]
        if layer + 1 < n_a:
            x, h = _out_proj(ys, a_w_out_bf, layer, x, a_norm_g[layer + 1:layer + 2], tm)
        else:
            x, h, h_kv = _out_proj(ys, a_w_out_bf, layer, x, jnp.stack([b_norm_g[0], kv_norm_g]), tm)

    wa = jnp.pad(kv_w_a, ((0, 0), (0, LANES - QK_ROPE))).astype(BF16)
    kv_w = kv_w_b.reshape(KV_LORA, N_HEADS, QK_NOPE + V_HEAD)
    wk = kv_w[:, :, :QK_NOPE].reshape(KV_LORA, N_HEADS * QK_NOPE).astype(BF16)
    wvt = kv_w[:, :, QK_NOPE:].reshape(KV_LORA, N_HEADS * V_HEAD).T.astype(BF16)
    gkn = k_norm_g[:QK_NOPE].reshape(1, QK_NOPE)
    gkr = jnp.pad(k_norm_g[QK_NOPE:], (0, LANES - QK_ROPE)).reshape(1, LANES)
    k, vt = _shared_kv(h_kv, wa, kv_latent_g.reshape(1, KV_LORA), wk, wvt, gkn, gkr,
                       cos_k, sin_k, tm, tk)

    q_scale = (QK_HEAD ** -0.5) * math.log2(math.e)
    b_w_in_bf = b_w_in.astype(BF16)
    b_w_qt = jnp.swapaxes(b_w_q_b, 1, 2).astype(BF16)
    b_w_out_bf = b_w_out.astype(BF16)[:, None]
    b_glat = b_q_latent_g.reshape(n_b, 1, Q_LORA)
    b_gq = b_q_norm_g.reshape(n_b, QK_HEAD, 1)
    for j in range(n_b):
        qt, sg = _q_proj(h, b_w_in_bf, b_glat, b_w_qt, b_gq, j, cos_t, sin_t, tm, tq, q_scale)
        score_bound = 1.01 * QK_HEAD * q_scale * jnp.max(jnp.abs(b_q_norm_g[j])) * jnp.max(jnp.abs(k_norm_g))
        og = lax.cond(score_bound <= ATT_BOUNDED_SCORE_LIMIT,
                      functools.partial(_attention, tk=tk, running_max=False),
                      functools.partial(_attention, tk=tk, running_max=True),
                      qt, k, vt, sg)
        if j + 1 < n_b:
            x, h = _out_proj([og], b_w_out_bf, j, x, b_norm_g[j + 1:j + 2], tm)
        else:
            (x,) = _out_proj([og], b_w_out_bf, j, x, None, tm)
    return x
```

```python
import functools
import math

import jax
import jax.numpy as jnp
from jax import lax
from jax.experimental import pallas as pl
from jax.experimental.pallas import tpu as pltpu

CHUNK = 64
POOL_WINDOWS = (2, 4, 8, 16)
N_HEADS = 16
QK_NOPE = 128
QK_ROPE = 64
QK_HEAD = QK_NOPE + QK_ROPE
V_HEAD = 128
KV_LORA = 512
Q_LORA = 512
ROPE_THETA = 10000.0
EPS = 1e-6

LANES = 128
POOL_HALO = 32
ATT_SUB = 256
V_ONES = 16
V_ROWS = V_HEAD + V_ONES
ATT_BOUNDED_SCORE_LIMIT = 64.0
VMEM_LIMIT_BYTES = 56 * 1024 * 1024

F32 = jnp.float32
BF16 = jnp.bfloat16
NT_DIMS = (((1,), (1,)), ((), ()))


def _cparams(n_axes):
    return pltpu.CompilerParams(
        dimension_semantics=("arbitrary",) * n_axes,
        vmem_limit_bytes=VMEM_LIMIT_BYTES)


def _resident(shape, index_map):
    return pl.BlockSpec(shape, index_map, pipeline_mode=pl.Buffered(1))


def _silu(v):
    return v * jax.nn.sigmoid(v)


def _fold_kernel(wu_ref, wgrp_ref, o_ref):
    o_ref[...] = jnp.dot(wu_ref[...].astype(BF16), wgrp_ref[...].astype(BF16),
                         preferred_element_type=F32).astype(BF16)


def _fold_group_weights(w_in, w_group, layer, tr):
    d = w_in.shape[1]
    _, n_groups, ga, _ = w_group.shape
    return pl.pallas_call(
        _fold_kernel,
        grid=(n_groups, d // tr),
        in_specs=[pl.BlockSpec((None, tr, ga), lambda g, r: (layer, r, g)),
                  pl.BlockSpec((None, None, ga, ga), lambda g, r: (layer, g, 0, 0))],
        out_specs=pl.BlockSpec((tr, ga), lambda g, r: (r, g)),
        out_shape=jax.ShapeDtypeStruct((d, n_groups * ga), BF16),
        compiler_params=_cparams(2),
        name="fold_group_weights",
    )(w_in, w_group)


def _pool_kernel(*refs, tm, levels, normalize):
    if normalize:
        h_ref, g_ref, wu_ref, wg_ref, sc_ref, y_ref, a_ref, b_ref, c_ref, wgb_ref, hn_ref = refs
    else:
        h_ref, wu_ref, wg_ref, sc_ref, y_ref, a_ref, b_ref, c_ref, wgb_ref = refs
    r = pl.program_id(1)
    ga = a_ref.shape[1]

    @pl.when((r == 0) & (pl.program_id(0) == 0))
    def _():
        wgb_ref[...] = wg_ref[...].astype(BF16)

    @pl.when(r == 0)
    def _():
        a_ref[0:POOL_HALO, :] = jnp.zeros((POOL_HALO, ga), F32)

    if normalize:
        x = h_ref[0]
        ms = jnp.mean(x * x, axis=-1, keepdims=True)
        hn_ref[...] = (x * lax.rsqrt(ms + EPS) * g_ref[...]).astype(BF16)
        h = hn_ref[...]
    else:
        h = h_ref[0]
    a_ref[POOL_HALO:POOL_HALO + tm, :] = jnp.dot(h, wu_ref[...], preferred_element_type=F32)

    src = a_ref
    for j in range(levels):
        shift = 1 << j
        lo = 8 * (j + 1)
        n = POOL_HALO + tm - lo
        dst = (b_ref, c_ref)[j % 2]
        dst[lo:lo + n, :] = src[lo:lo + n, :] + src[lo - shift:lo - shift + n, :]
        src = dst

    window = 1 << levels
    t = r * tm + lax.broadcasted_iota(jnp.int32, (tm, LANES), 0)
    inv_count = 1.0 / jnp.minimum(t + 1, window).astype(F32)
    gate = jnp.dot(h, wgb_ref[...], preferred_element_type=F32)
    for c in range(ga // LANES):
        cols = slice(c * LANES, (c + 1) * LANES)
        z = src[POOL_HALO:POOL_HALO + tm, cols] * inv_count - a_ref[POOL_HALO:POOL_HALO + tm, cols]
        y_ref[0, :, cols] = (z * sc_ref[:, cols] * _silu(gate[:, cols])).astype(BF16)

    a_ref[0:POOL_HALO, :] = a_ref[tm:tm + POOL_HALO, :]


def _pool_group(h, w_fold, w_in, scale, layer, g, tm, norm_gain=None):
    b, s, d = h.shape
    normalize = norm_gain is not None
    n_groups = len(POOL_WINDOWS)
    ga = w_fold.shape[1] // n_groups
    levels = int(math.log2(POOL_WINDOWS[g]))
    assert 1 << levels == POOL_WINDOWS[g] and 8 * levels <= POOL_HALO <= tm
    gain_spec = [_resident((1, d), lambda i, r: (0, 0))] if normalize else []
    gain_arg = [norm_gain] if normalize else []
    hn_scratch = [pltpu.VMEM((tm, d), BF16)] if normalize else []
    return pl.pallas_call(
        functools.partial(_pool_kernel, tm=tm, levels=levels, normalize=normalize),
        grid=(b, s // tm),
        in_specs=[pl.BlockSpec((1, tm, d), lambda i, r: (i, r, 0)),
                  *gain_spec,
                  _resident((d, ga), lambda i, r: (0, g)),
                  _resident((None, d, ga), lambda i, r: (layer, 0, n_groups + g)),
                  _resident((None, 1, ga), lambda i, r: (layer, 0, g))],
        out_specs=pl.BlockSpec((1, tm, ga), lambda i, r: (i, r, 0)),
        out_shape=jax.ShapeDtypeStruct((b, s, ga), BF16),
        scratch_shapes=[pltpu.VMEM((POOL_HALO + tm, ga), F32),
                        pltpu.VMEM((POOL_HALO + tm, ga), F32),
                        pltpu.VMEM((POOL_HALO + tm, ga), F32),
                        pltpu.VMEM((d, ga), BF16),
                        *hn_scratch],
        compiler_params=_cparams(2),
        name=f"pool_g{g}_norm" if normalize else f"pool_g{g}",
    )(h, *gain_arg, w_fold, w_in, scale)


def _out_kernel(*refs, n_parts, n_norm):
    y_refs = refs[:n_parts]
    w_ref, x_ref = refs[n_parts], refs[n_parts + 1]
    g_ref = refs[n_parts + 2] if n_norm else None
    outs = refs[n_parts + 2 + (1 if n_norm else 0):]
    xo_ref, h_refs = outs[0], outs[1:]

    acc = x_ref[0]
    for p in range(n_parts):
        if len(y_refs[p].shape) == 4:
            y = jnp.concatenate([y_refs[p][0, hd] for hd in range(y_refs[p].shape[1])], axis=1)
        else:
            y = y_refs[p][0]
        acc = acc + jnp.dot(y, w_ref[p], preferred_element_type=F32)
    xo_ref[0] = acc
    if n_norm:
        ms = jnp.mean(acc * acc, axis=-1, keepdims=True)
        rn = acc * lax.rsqrt(ms + EPS)
        for j in range(n_norm):
            h_refs[j][0] = (rn * g_ref[j:j + 1, :]).astype(BF16)


def _out_proj(parts, w, layer, x, gains, tm):
    b, s, d = x.shape
    _, n_parts, kp, _ = w.shape
    n_norm = 0 if gains is None else gains.shape[0]
    row = lambda i, r: (i, r, 0)
    in_specs = [pl.BlockSpec((1, tm, kp), row) if part.ndim == 3 else
                pl.BlockSpec((1, part.shape[1], tm, part.shape[3]), lambda i, r: (i, 0, r, 0))
                for part in parts]
    in_specs += [_resident((None, n_parts, kp, d), lambda i, r: (layer, 0, 0, 0)),
                 pl.BlockSpec((1, tm, d), row)]
    args = list(parts) + [w, x]
    if n_norm:
        in_specs.append(_resident((n_norm, d), lambda i, r: (0, 0)))
        args.append(gains)
    out_specs = [pl.BlockSpec((1, tm, d), row) for _ in range(1 + n_norm)]
    out_shape = [jax.ShapeDtypeStruct((b, s, d), F32)]
    out_shape += [jax.ShapeDtypeStruct((b, s, d), BF16) for _ in range(n_norm)]
    return pl.pallas_call(
        functools.partial(_out_kernel, n_parts=n_parts, n_norm=n_norm),
        grid=(b, s // tm),
        in_specs=in_specs,
        out_specs=out_specs,
        out_shape=out_shape,
        compiler_params=_cparams(2),
        name=f"out_proj_p{n_parts}_n{n_norm}",
    )(*args)


def _kv_kernel(h_ref, wa_ref, glat_ref, wk_ref, wvt_ref, gkn_ref, gkr_ref, cos_ref, sin_ref,
               k_ref, vt_ref, *, tm, tkv):
    h = h_ref[0]
    ckv = jnp.dot(h, wa_ref[...], preferred_element_type=F32)
    c = ckv[:, :KV_LORA]
    kr = ckv[:, KV_LORA:]
    cn = (c * lax.rsqrt(jnp.mean(c * c, axis=-1, keepdims=True) + EPS) * glat_ref[...]).astype(BF16)
    kn = jnp.dot(cn, wk_ref[...], preferred_element_type=F32)
    vt = lax.dot_general(wvt_ref[...], cn, NT_DIMS, preferred_element_type=F32)

    ss_rope = jnp.sum(kr * kr, axis=-1, keepdims=True)
    krg = kr * gkr_ref[...]
    half = QK_ROPE // 2
    swapped = pltpu.roll(krg, half, 1) + pltpu.roll(krg, LANES - half, 1)
    roped = krg * cos_ref[0] + swapped * sin_ref[0]

    for hd in range(N_HEADS):
        knh = kn[:, hd * QK_NOPE:(hd + 1) * QK_NOPE]
        ss = jnp.sum(knh * knh, axis=-1, keepdims=True) + ss_rope
        rinv = lax.rsqrt(ss * (1.0 / QK_HEAD) + EPS)
        k_ref[0, hd, :, 0:QK_NOPE] = (knh * rinv * gkn_ref[...]).astype(BF16)
        k_ref[0, hd, :, QK_NOPE:QK_HEAD] = (roped[:, :QK_ROPE] * rinv).astype(BF16)
        for j in range(tm // tkv):
            vt_ref[0, hd, j, 0:V_HEAD, :] = vt[hd * V_HEAD:(hd + 1) * V_HEAD, j * tkv:(j + 1) * tkv].astype(BF16)
            vt_ref[0, hd, j, V_HEAD:V_ROWS, :] = jnp.ones((V_ONES, tkv), BF16)


def _shared_kv(h, wa, glat, wk, wvt, gkn, gkr, cos_k, sin_k, tm, tkv):
    b, s, d = h.shape
    row = lambda i, r: (i, r, 0)
    const2 = lambda i, r: (0, 0)
    return pl.pallas_call(
        functools.partial(_kv_kernel, tm=tm, tkv=tkv),
        grid=(b, s // tm),
        in_specs=[pl.BlockSpec((1, tm, d), row),
                  _resident(wa.shape, const2),
                  _resident(glat.shape, const2),
                  _resident(wk.shape, const2),
                  _resident(wvt.shape, const2),
                  _resident(gkn.shape, const2),
                  _resident(gkr.shape, const2),
                  pl.BlockSpec((1, tm, LANES), row),
                  pl.BlockSpec((1, tm, LANES), row)],
        out_specs=[pl.BlockSpec((1, N_HEADS, tm, QK_HEAD), lambda i, r: (i, 0, r, 0)),
                   pl.BlockSpec((1, N_HEADS, tm // tkv, V_ROWS, tkv), lambda i, r: (i, 0, r, 0, 0))],
        out_shape=[jax.ShapeDtypeStruct((b, N_HEADS, s, QK_HEAD), BF16),
                   jax.ShapeDtypeStruct((b, N_HEADS, s // tkv, V_ROWS, tkv), BF16)],
        compiler_params=_cparams(2),
        name="shared_kv",
    )(h, wa, glat, wk, wvt, gkn, gkr, cos_k, sin_k)


def _q_kernel(*refs, tm, q_scale, n_gate):
    h_ref, wlat_ref, glat_ref = refs[:3]
    wgate_refs = refs[3:3 + n_gate]
    wqt_ref, gq_ref, cos_ref, sin_ref, qt_ref, sg_ref = refs[3 + n_gate:]
    h = h_ref[0]
    ql = jnp.dot(h, wlat_ref[...], preferred_element_type=F32)
    qn = (ql * lax.rsqrt(jnp.mean(ql * ql, axis=-1, keepdims=True) + EPS) * glat_ref[...]).astype(BF16)
    for c, wgate_ref in enumerate(wgate_refs):
        gate = jnp.dot(h, wgate_ref[...], preferred_element_type=F32)
        sg = _silu(gate).astype(BF16)
        for i in range(Q_LORA // V_HEAD):
            sg_ref[0, c * (Q_LORA // V_HEAD) + i] = sg[:, i * V_HEAD:(i + 1) * V_HEAD]

    qt = lax.dot_general(wqt_ref[...], qn, NT_DIMS, preferred_element_type=F32)
    gq = jnp.broadcast_to(gq_ref[...], (QK_HEAD, tm))
    cos = cos_ref[0]
    sin = sin_ref[0]
    half = QK_ROPE // 2
    for hd in range(N_HEADS):
        q = qt[hd * QK_HEAD:(hd + 1) * QK_HEAD, :]
        ss = jnp.sum(q * q, axis=0, keepdims=True)
        rinv = lax.rsqrt(ss * (1.0 / QK_HEAD) + EPS) * q_scale
        qg = q * gq
        t1 = qg[QK_NOPE:QK_NOPE + half]
        t2 = qg[QK_NOPE + half:]
        qt_ref[0, hd, 0, 0:QK_NOPE, :] = (qg[:QK_NOPE] * rinv).astype(BF16)
        qt_ref[0, hd, 0, QK_NOPE:QK_NOPE + half, :] = ((t1 * cos - t2 * sin) * rinv).astype(BF16)
        qt_ref[0, hd, 0, QK_NOPE + half:QK_HEAD, :] = ((t2 * cos + t1 * sin) * rinv).astype(BF16)


def _q_proj(h, w_in, glat, wqt, gq, layer, cos_t, sin_t, tm, tq, q_scale):
    b, s, d = h.shape
    per_q = tq // tm
    of_layer = lambda i, r: (layer, 0, 0)
    per_layer = lambda a: _resident((None,) + a.shape[1:], of_layer)
    n_gate = w_in.shape[2] // Q_LORA - 1
    wb = n_gate * Q_LORA
    w_in_block = lambda c: _resident((None, d, Q_LORA), lambda i, r: (layer, 0, c))
    return pl.pallas_call(
        functools.partial(_q_kernel, tm=tm, q_scale=q_scale, n_gate=n_gate),
        grid=(b, s // tm),
        in_specs=[pl.BlockSpec((1, tm, d), lambda i, r: (i, r, 0)),
                  w_in_block(0), per_layer(glat), *[w_in_block(1 + c) for c in range(n_gate)],
                  per_layer(wqt), per_layer(gq),
                  pl.BlockSpec((1, QK_ROPE // 2, tm), lambda i, r: (i, 0, r)),
                  pl.BlockSpec((1, QK_ROPE // 2, tm), lambda i, r: (i, 0, r))],
        out_specs=[pl.BlockSpec((1, N_HEADS, 1, QK_HEAD, tm), lambda i, r: (i, 0, r // per_q, 0, r % per_q)),
                   pl.BlockSpec((1, wb // V_HEAD, tm, V_HEAD), lambda i, r: (i, 0, r, 0))],
        out_shape=[jax.ShapeDtypeStruct((b, N_HEADS, s // tq, QK_HEAD, tq), BF16),
                   jax.ShapeDtypeStruct((b, wb // V_HEAD, s, V_HEAD), BF16)],
        compiler_params=_cparams(2),
        name="q_proj",
    )(h, w_in, glat, *([w_in] * n_gate), wqt, gq, cos_t, sin_t)


def _attn_kernel(qt_ref, k_ref, vt_ref, sg_ref, o_ref, *scratch, tq, tk, running_max):
    n_q = qt_ref.shape[2]
    n_chain = tq // ATT_SUB
    n_per = tq // tk
    sub_per = tk // ATT_SUB
    if running_max:
        s0_ref, s1_ref, cm0_ref, cm1_ref, m_ref, acc_ref = scratch
        cm_refs = (cm0_ref, cm1_ref)
    else:
        s0_ref, s1_ref, ps0_ref, ps1_ref, l_ref, acc_ref = scratch
        ps_refs = (ps0_ref, ps1_ref)
    s_refs = (s0_ref, s1_ref)

    def chain_cols(j):
        return slice(j * ATT_SUB, (j + 1) * ATT_SUB)

    items = []
    for qi in range(n_q):
        items += [(qi, kt, None) for kt in range(qi * n_per)]
        items += [(qi, qi * n_per + d, d) for d in range(n_per)]

    def visible_rows(item, j):
        _, _, d = item
        if d is None:
            return tk
        return min(max(j - d * sub_per + 1, 0), sub_per) * ATT_SUB

    def issue_scores(t, j):
        qi, kt, d = items[t]
        rows = visible_rows(items[t], j)
        if rows == 0:
            return
        k_tile = k_ref[0, 0, kt * tk:kt * tk + rows, :]
        st = jnp.dot(k_tile, qt_ref[0, 0, qi, :, chain_cols(j)], preferred_element_type=F32)
        if d is not None and j < (d + 1) * sub_per:
            k_chunk = lax.broadcasted_iota(jnp.int32, (rows, ATT_SUB), 0) // CHUNK
            q_chunk = lax.broadcasted_iota(jnp.int32, (rows, ATT_SUB), 1) // CHUNK
            st = jnp.where(k_chunk <= q_chunk + (rows // ATT_SUB - 1) * (ATT_SUB // CHUNK), st, -1e30)
        if running_max:
            s_refs[t % 2][j, 0:rows, :] = st
            cm_refs[t % 2][j] = jnp.max(st, axis=0, keepdims=True)
        else:
            p = jnp.exp2(st)
            s_refs[t % 2][j, 0:rows, :] = p.astype(BF16)
            ps_refs[t % 2][j] = jnp.sum(p, axis=0, keepdims=True)

    def consume(t, j):
        qi, kt, _ = items[t]
        rows = visible_rows(items[t], j)
        if rows == 0:
            return
        cols = chain_cols(j)
        vt_tile = vt_ref[0, 0, kt, :, 0:rows]
        if running_max:
            m_old = m_ref[qi, :, cols]
            m_new = jnp.maximum(m_old, cm_refs[t % 2][j])
            m_ref[qi, :, cols] = m_new
            p = jnp.exp2(s_refs[t % 2][j, 0:rows, :] - m_new).astype(BF16)
            pv = jnp.dot(vt_tile, p, preferred_element_type=F32)
            acc_ref[qi, :, cols] = jnp.exp2(m_old - m_new) * acc_ref[qi, :, cols] + pv
        else:
            l_ref[qi, :, cols] += ps_refs[t % 2][j]
            acc_ref[qi, :, cols] += jnp.dot(vt_tile[0:V_HEAD], s_refs[t % 2][j, 0:rows, :],
                                            preferred_element_type=F32)

    for j in range(n_chain):
        issue_scores(0, j)
    for t, (qi, kt, d) in enumerate(items):
        if t == 0 or items[t - 1][0] != qi:
            if running_max:
                m_ref[qi] = jnp.full(m_ref.shape[1:], -1e30, F32)
            else:
                l_ref[qi] = jnp.zeros(l_ref.shape[1:], F32)
            acc_ref[qi] = jnp.zeros(acc_ref.shape[1:], F32)
        for j in range(n_chain):
            if t + 1 < len(items):
                issue_scores(t + 1, j)
            consume(t, j)
        if t + 1 == len(items) or items[t + 1][0] != qi:
            inv_l = 1.0 / (acc_ref[qi, V_HEAD:V_HEAD + 1, :] if running_max else l_ref[qi])
            o = (acc_ref[qi, 0:V_HEAD, :] * inv_l).T
            rows_q = slice(qi * tq, (qi + 1) * tq)
            o_ref[0, 0, rows_q, :] = (o * sg_ref[0, 0, rows_q, :].astype(F32)).astype(BF16)


def _attention(qt, k, vt, sg, *, tk, running_max):
    b, n_heads, n_q, _, tq = qt.shape
    s = n_q * tq
    n_kt = s // tk
    n_chain = tq // ATT_SUB
    assert tq % tk == 0 and tk % ATT_SUB == 0
    if running_max:
        scratch_shapes = [pltpu.VMEM((n_chain, tk, ATT_SUB), F32),
                          pltpu.VMEM((n_chain, tk, ATT_SUB), F32),
                          pltpu.VMEM((n_chain, 1, ATT_SUB), F32),
                          pltpu.VMEM((n_chain, 1, ATT_SUB), F32),
                          pltpu.VMEM((n_q, 1, tq), F32),
                          pltpu.VMEM((n_q, V_ROWS, tq), F32)]
    else:
        scratch_shapes = [pltpu.VMEM((n_chain, tk, ATT_SUB), BF16),
                          pltpu.VMEM((n_chain, tk, ATT_SUB), BF16),
                          pltpu.VMEM((n_chain, 1, ATT_SUB), F32),
                          pltpu.VMEM((n_chain, 1, ATT_SUB), F32),
                          pltpu.VMEM((n_q, 1, tq), F32),
                          pltpu.VMEM((n_q, V_HEAD, tq), F32)]
    return pl.pallas_call(
        functools.partial(_attn_kernel, tq=tq, tk=tk, running_max=running_max),
        grid=(b, n_heads),
        in_specs=[pl.BlockSpec((1, 1, n_q, QK_HEAD, tq), lambda i, hd: (i, hd, 0, 0, 0)),
                  pl.BlockSpec((1, 1, s, QK_HEAD), lambda i, hd: (i, hd, 0, 0)),
                  pl.BlockSpec((1, 1, n_kt, V_ROWS, tk), lambda i, hd: (i, hd, 0, 0, 0)),
                  pl.BlockSpec((1, 1, s, V_HEAD), lambda i, hd: (i, hd, 0, 0))],
        out_specs=pl.BlockSpec((1, 1, s, V_HEAD), lambda i, hd: (i, hd, 0, 0)),
        out_shape=jax.ShapeDtypeStruct((b, n_heads, s, V_HEAD), BF16),
        scratch_shapes=scratch_shapes,
        compiler_params=_cparams(2),
        name="attn" if running_max else "attn_bounded",
    )(qt, k, vt, sg)


def _rope_tables(positions):
    inv = ROPE_THETA ** (-jnp.arange(0, QK_ROPE, 2, dtype=F32) / QK_ROPE)
    ang = inv[None, :, None] * positions.astype(F32)[:, None, :]
    return lax.optimization_barrier((jnp.cos(ang), jnp.sin(ang)))


def kernel(x, positions, a_norm_g, a_w_in, a_w_group, a_scale, a_w_out, kv_norm_g, kv_w_a, kv_latent_g, kv_w_b, k_norm_g, b_norm_g, b_w_in, b_q_latent_g, b_w_q_b, b_q_norm_g, b_w_out):
    b, s, d = x.shape
    n_a = a_w_in.shape[0]
    n_b = b_w_in.shape[0]
    n_groups, ga = a_w_group.shape[1], a_w_group.shape[2]
    tm = min(512, s)
    tm_pool = min(1024, s)
    tq = min(1024, s)
    tk = min(512, s)

    cos_t, sin_t = _rope_tables(positions)
    cos = jnp.swapaxes(cos_t, 1, 2)
    sin = jnp.swapaxes(sin_t, 1, 2)
    zeros = jnp.zeros((b, s, LANES - QK_ROPE), F32)
    cos_k = jnp.concatenate([cos, cos, zeros], axis=-1)
    sin_k = jnp.concatenate([-sin, sin, zeros], axis=-1)

    a_w_out_bf = a_w_out.astype(BF16).reshape(n_a, n_groups, ga, d)
    a_scale3 = a_scale.reshape(n_a, 1, n_groups * ga)

    h = None
    for layer in range(n_a):
        w_fold = _fold_group_weights(a_w_in, a_w_group, layer, min(1024, d))
        if layer == 0:
            ys = [_pool_group(x, w_fold, a_w_in, a_scale3, layer, g, tm, norm_gain=a_norm_g[0:1])
                  for g in range(n_groups)]
        else:
            ys = [_pool_group(h, w_fold, a_w_in, a_scale3, layer, g, tm_pool) for g in range(n_groups)]
        if layer + 1 < n_a:
            x, h = _out_proj(ys, a_w_out_bf, layer, x, a_norm_g[layer + 1:layer + 2], tm)
        else:
            x, h, h_kv = _out_proj(ys, a_w_out_bf, layer, x, jnp.stack([b_norm_g[0], kv_norm_g]), tm)

    wa = jnp.pad(kv_w_a, ((0, 0), (0, LANES - QK_ROPE))).astype(BF16)
    kv_w = kv_w_b.reshape(KV_LORA, N_HEADS, QK_NOPE + V_HEAD)
    wk = kv_w[:, :, :QK_NOPE].reshape(KV_LORA, N_HEADS * QK_NOPE).astype(BF16)
    wvt = kv_w[:, :, QK_NOPE:].reshape(KV_LORA, N_HEADS * V_HEAD).T.astype(BF16)
    gkn = k_norm_g[:QK_NOPE].reshape(1, QK_NOPE)
    gkr = jnp.pad(k_norm_g[QK_NOPE:], (0, LANES - QK_ROPE)).reshape(1, LANES)
    k, vt = _shared_kv(h_kv, wa, kv_latent_g.reshape(1, KV_LORA), wk, wvt, gkn, gkr,
                       cos_k, sin_k, tm, tk)

    q_scale = (QK_HEAD ** -0.5) * math.log2(math.e)
    b_w_in_bf = b_w_in.astype(BF16)
    b_w_qt = jnp.swapaxes(b_w_q_b, 1, 2).astype(BF16)
    b_w_out_bf = b_w_out.astype(BF16)[:, None]
    b_glat = b_q_latent_g.reshape(n_b, 1, Q_LORA)
    b_gq = b_q_norm_g.reshape(n_b, QK_HEAD, 1)
    for j in range(n_b):
        qt, sg = _q_proj(h, b_w_in_bf, b_glat, b_w_qt, b_gq, j, cos_t, sin_t, tm, tq, q_scale)
        score_bound = 1.01 * QK_HEAD * q_scale * jnp.max(jnp.abs(b_q_norm_g[j])) * jnp.max(jnp.abs(k_norm_g))
        og = lax.cond(score_bound <= ATT_BOUNDED_SCORE_LIMIT,
                      functools.partial(_attention, tk=tk, running_max=False),
                      functools.partial(_attention, tk=tk, running_max=True),
                      qt, k, vt, sg)
        if j + 1 < n_b:
            x, h = _out_proj([og], b_w_out_bf, j, x, b_norm_g[j + 1:j + 2], tm)
        else:
            (x,) = _out_proj([og], b_w_out_bf, j, x, None, tm)
    return x
```

```python
import functools
import math

import jax
import jax.numpy as jnp
from jax import lax
from jax.experimental import pallas as pl
from jax.experimental.pallas import tpu as pltpu

CHUNK = 64
POOL_WINDOWS = (2, 4, 8, 16)
N_HEADS = 16
QK_NOPE = 128
QK_ROPE = 64
QK_HEAD = QK_NOPE + QK_ROPE
V_HEAD = 128
KV_LORA = 512
Q_LORA = 512
ROPE_THETA = 10000.0
EPS = 1e-6

LANES = 128
POOL_HALO = 32
ATT_SUB = 256
V_ONES = 16
V_ROWS = V_HEAD + V_ONES
ATT_BOUNDED_SCORE_LIMIT = 64.0
VMEM_LIMIT_BYTES = 56 * 1024 * 1024

F32 = jnp.float32
BF16 = jnp.bfloat16
NT_DIMS = (((1,), (1,)), ((), ()))


def _cparams(n_axes):
    return pltpu.CompilerParams(
        dimension_semantics=("arbitrary",) * n_axes,
        vmem_limit_bytes=VMEM_LIMIT_BYTES)


def _resident(shape, index_map):
    return pl.BlockSpec(shape, index_map, pipeline_mode=pl.Buffered(1))


def _silu(v):
    return v * jax.nn.sigmoid(v)


def _fold_kernel(wu_ref, wgrp_ref, o_ref):
    o_ref[...] = jnp.dot(wu_ref[...].astype(BF16), wgrp_ref[...].astype(BF16),
                         preferred_element_type=F32).astype(BF16)


def _fold_group_weights(w_in, w_group, layer, tr):
    d = w_in.shape[1]
    _, n_groups, ga, _ = w_group.shape
    return pl.pallas_call(
        _fold_kernel,
        grid=(n_groups, d // tr),
        in_specs=[pl.BlockSpec((None, tr, ga), lambda g, r: (layer, r, g)),
                  pl.BlockSpec((None, None, ga, ga), lambda g, r: (layer, g, 0, 0))],
        out_specs=pl.BlockSpec((tr, ga), lambda g, r: (r, g)),
        out_shape=jax.ShapeDtypeStruct((d, n_groups * ga), BF16),
        compiler_params=_cparams(2),
        name="fold_group_weights",
    )(w_in, w_group)


def _pool_kernel(*refs, tm, levels, normalize):
    if normalize:
        h_ref, g_ref, wu_ref, wg_ref, sc_ref, y_ref, hn_ref, a_ref, b_ref, c_ref, wgb_ref = refs
    else:
        h_ref, wu_ref, wg_ref, sc_ref, y_ref, a_ref, b_ref, c_ref, wgb_ref = refs
    r = pl.program_id(1)
    ga = a_ref.shape[1]

    @pl.when((r == 0) & (pl.program_id(0) == 0))
    def _():
        wgb_ref[...] = wg_ref[...].astype(BF16)

    @pl.when(r == 0)
    def _():
        a_ref[0:POOL_HALO, :] = jnp.zeros((POOL_HALO, ga), F32)

    if normalize:
        x = h_ref[0]
        ms = jnp.mean(x * x, axis=-1, keepdims=True)
        hn_ref[0] = (x * lax.rsqrt(ms + EPS) * g_ref[...]).astype(BF16)
        h = hn_ref[0]
    else:
        h = h_ref[0]
    a_ref[POOL_HALO:POOL_HALO + tm, :] = jnp.dot(h, wu_ref[...], preferred_element_type=F32)

    src = a_ref
    for j in range(levels):
        shift = 1 << j
        lo = 8 * (j + 1)
        n = POOL_HALO + tm - lo
        dst = (b_ref, c_ref)[j % 2]
        dst[lo:lo + n, :] = src[lo:lo + n, :] + src[lo - shift:lo - shift + n, :]
        src = dst

    window = 1 << levels
    t = r * tm + lax.broadcasted_iota(jnp.int32, (tm, LANES), 0)
    inv_count = 1.0 / jnp.minimum(t + 1, window).astype(F32)
    gate = jnp.dot(h, wgb_ref[...], preferred_element_type=F32)
    for c in range(ga // LANES):
        cols = slice(c * LANES, (c + 1) * LANES)
        z = src[POOL_HALO:POOL_HALO + tm, cols] * inv_count - a_ref[POOL_HALO:POOL_HALO + tm, cols]
        y_ref[0, :, cols] = (z * sc_ref[:, cols] * _silu(gate[:, cols])).astype(BF16)

    a_ref[0:POOL_HALO, :] = a_ref[tm:tm + POOL_HALO, :]


def _pool_group(h, w_fold, w_in, scale, layer, g, tm, norm_gain=None):
    b, s, d = h.shape
    normalize = norm_gain is not None
    n_groups = len(POOL_WINDOWS)
    ga = w_fold.shape[1] // n_groups
    levels = int(math.log2(POOL_WINDOWS[g]))
    assert 1 << levels == POOL_WINDOWS[g] and 8 * levels <= POOL_HALO <= tm
    gain_spec = [_resident((1, d), lambda i, r: (0, 0))] if normalize else []
    gain_arg = [norm_gain] if normalize else []
    out_specs = [pl.BlockSpec((1, tm, ga), lambda i, r: (i, r, 0))]
    out_shape = [jax.ShapeDtypeStruct((b, s, ga), BF16)]
    if normalize:
        out_specs.append(pl.BlockSpec((1, tm, d), lambda i, r: (i, r, 0)))
        out_shape.append(jax.ShapeDtypeStruct((b, s, d), BF16))
    outs = pl.pallas_call(
        functools.partial(_pool_kernel, tm=tm, levels=levels, normalize=normalize),
        grid=(b, s // tm),
        in_specs=[pl.BlockSpec((1, tm, d), lambda i, r: (i, r, 0)),
                  *gain_spec,
                  _resident((d, ga), lambda i, r: (0, g)),
                  _resident((None, d, ga), lambda i, r: (layer, 0, n_groups + g)),
                  _resident((None, 1, ga), lambda i, r: (layer, 0, g))],
        out_specs=out_specs,
        out_shape=out_shape,
        scratch_shapes=[pltpu.VMEM((POOL_HALO + tm, ga), F32),
                        pltpu.VMEM((POOL_HALO + tm, ga), F32),
                        pltpu.VMEM((POOL_HALO + tm, ga), F32),
                        pltpu.VMEM((d, ga), BF16)],
        compiler_params=_cparams(2),
        name=f"pool_g{g}_norm" if normalize else f"pool_g{g}",
    )(h, *gain_arg, w_fold, w_in, scale)
    return tuple(outs) if normalize else outs[0]


def _out_kernel(*refs, n_parts, n_norm):
    y_refs = refs[:n_parts]
    w_ref, x_ref = refs[n_parts], refs[n_parts + 1]
    g_ref = refs[n_parts + 2] if n_norm else None
    outs = refs[n_parts + 2 + (1 if n_norm else 0):]
    xo_ref, h_refs = outs[0], outs[1:]

    acc = x_ref[0]
    for p in range(n_parts):
        if len(y_refs[p].shape) == 4:
            y = jnp.concatenate([y_refs[p][0, hd] for hd in range(y_refs[p].shape[1])], axis=1)
        else:
            y = y_refs[p][0]
        acc = acc + jnp.dot(y, w_ref[p], preferred_element_type=F32)
    xo_ref[0] = acc
    if n_norm:
        ms = jnp.mean(acc * acc, axis=-1, keepdims=True)
        rn = acc * lax.rsqrt(ms + EPS)
        for j in range(n_norm):
            h_refs[j][0] = (rn * g_ref[j:j + 1, :]).astype(BF16)


def _out_proj(parts, w, layer, x, gains, tm):
    b, s, d = x.shape
    _, n_parts, kp, _ = w.shape
    n_norm = 0 if gains is None else gains.shape[0]
    row = lambda i, r: (i, r, 0)
    in_specs = [pl.BlockSpec((1, tm, kp), row) if part.ndim == 3 else
                pl.BlockSpec((1, part.shape[1], tm, part.shape[3]), lambda i, r: (i, 0, r, 0))
                for part in parts]
    in_specs += [_resident((None, n_parts, kp, d), lambda i, r: (layer, 0, 0, 0)),
                 pl.BlockSpec((1, tm, d), row)]
    args = list(parts) + [w, x]
    if n_norm:
        in_specs.append(_resident((n_norm, d), lambda i, r: (0, 0)))
        args.append(gains)
    out_specs = [pl.BlockSpec((1, tm, d), row) for _ in range(1 + n_norm)]
    out_shape = [jax.ShapeDtypeStruct((b, s, d), F32)]
    out_shape += [jax.ShapeDtypeStruct((b, s, d), BF16) for _ in range(n_norm)]
    return pl.pallas_call(
        functools.partial(_out_kernel, n_parts=n_parts, n_norm=n_norm),
        grid=(b, s // tm),
        in_specs=in_specs,
        out_specs=out_specs,
        out_shape=out_shape,
        compiler_params=_cparams(2),
        name=f"out_proj_p{n_parts}_n{n_norm}",
    )(*args)


def _kv_kernel(h_ref, wa_ref, glat_ref, wk_ref, wvt_ref, gkn_ref, gkr_ref, cos_ref, sin_ref,
               k_ref, vt_ref, *, tm, tkv):
    h = h_ref[0]
    ckv = jnp.dot(h, wa_ref[...], preferred_element_type=F32)
    c = ckv[:, :KV_LORA]
    kr = ckv[:, KV_LORA:]
    cn = (c * lax.rsqrt(jnp.mean(c * c, axis=-1, keepdims=True) + EPS) * glat_ref[...]).astype(BF16)
    kn = jnp.dot(cn, wk_ref[...], preferred_element_type=F32)
    vt = lax.dot_general(wvt_ref[...], cn, NT_DIMS, preferred_element_type=F32)

    ss_rope = jnp.sum(kr * kr, axis=-1, keepdims=True)
    krg = kr * gkr_ref[...]
    half = QK_ROPE // 2
    swapped = pltpu.roll(krg, half, 1) + pltpu.roll(krg, LANES - half, 1)
    roped = krg * cos_ref[0] + swapped * sin_ref[0]

    for hd in range(N_HEADS):
        knh = kn[:, hd * QK_NOPE:(hd + 1) * QK_NOPE]
        ss = jnp.sum(knh * knh, axis=-1, keepdims=True) + ss_rope
        rinv = lax.rsqrt(ss * (1.0 / QK_HEAD) + EPS)
        k_ref[0, hd, :, 0:QK_NOPE] = (knh * rinv * gkn_ref[...]).astype(BF16)
        k_ref[0, hd, :, QK_NOPE:QK_HEAD] = (roped[:, :QK_ROPE] * rinv).astype(BF16)
        for j in range(tm // tkv):
            vt_ref[0, hd, j, 0:V_HEAD, :] = vt[hd * V_HEAD:(hd + 1) * V_HEAD, j * tkv:(j + 1) * tkv].astype(BF16)
            vt_ref[0, hd, j, V_HEAD:V_ROWS, :] = jnp.ones((V_ONES, tkv), BF16)


def _shared_kv(h, wa, glat, wk, wvt, gkn, gkr, cos_k, sin_k, tm, tkv):
    b, s, d = h.shape
    row = lambda i, r: (i, r, 0)
    const2 = lambda i, r: (0, 0)
    return pl.pallas_call(
        functools.partial(_kv_kernel, tm=tm, tkv=tkv),
        grid=(b, s // tm),
        in_specs=[pl.BlockSpec((1, tm, d), row),
                  _resident(wa.shape, const2),
                  _resident(glat.shape, const2),
                  _resident(wk.shape, const2),
                  _resident(wvt.shape, const2),
                  _resident(gkn.shape, const2),
                  _resident(gkr.shape, const2),
                  pl.BlockSpec((1, tm, LANES), row),
                  pl.BlockSpec((1, tm, LANES), row)],
        out_specs=[pl.BlockSpec((1, N_HEADS, tm, QK_HEAD), lambda i, r: (i, 0, r, 0)),
                   pl.BlockSpec((1, N_HEADS, tm // tkv, V_ROWS, tkv), lambda i, r: (i, 0, r, 0, 0))],
        out_shape=[jax.ShapeDtypeStruct((b, N_HEADS, s, QK_HEAD), BF16),
                   jax.ShapeDtypeStruct((b, N_HEADS, s // tkv, V_ROWS, tkv), BF16)],
        compiler_params=_cparams(2),
        name="shared_kv",
    )(h, wa, glat, wk, wvt, gkn, gkr, cos_k, sin_k)


def _q_kernel(*refs, tm, q_scale, n_gate):
    h_ref, wlat_ref, glat_ref = refs[:3]
    wgate_refs = refs[3:3 + n_gate]
    wqt_ref, gq_ref, cos_ref, sin_ref, qt_ref, sg_ref = refs[3 + n_gate:]
    h = h_ref[0]
    ql = jnp.dot(h, wlat_ref[...], preferred_element_type=F32)
    qn = (ql * lax.rsqrt(jnp.mean(ql * ql, axis=-1, keepdims=True) + EPS) * glat_ref[...]).astype(BF16)
    for c, wgate_ref in enumerate(wgate_refs):
        gate = jnp.dot(h, wgate_ref[...], preferred_element_type=F32)
        sg = _silu(gate).astype(BF16)
        for i in range(Q_LORA // V_HEAD):
            sg_ref[0, c * (Q_LORA // V_HEAD) + i] = sg[:, i * V_HEAD:(i + 1) * V_HEAD]

    qt = lax.dot_general(wqt_ref[...], qn, NT_DIMS, preferred_element_type=F32)
    gq = jnp.broadcast_to(gq_ref[...], (QK_HEAD, tm))
    cos = cos_ref[0]
    sin = sin_ref[0]
    half = QK_ROPE // 2
    for hd in range(N_HEADS):
        q = qt[hd * QK_HEAD:(hd + 1) * QK_HEAD, :]
        ss = jnp.sum(q * q, axis=0, keepdims=True)
        rinv = lax.rsqrt(ss * (1.0 / QK_HEAD) + EPS) * q_scale
        qg = q * gq
        t1 = qg[QK_NOPE:QK_NOPE + half]
        t2 = qg[QK_NOPE + half:]
        qt_ref[0, hd, 0, 0:QK_NOPE, :] = (qg[:QK_NOPE] * rinv).astype(BF16)
        qt_ref[0, hd, 0, QK_NOPE:QK_NOPE + half, :] = ((t1 * cos - t2 * sin) * rinv).astype(BF16)
        qt_ref[0, hd, 0, QK_NOPE + half:QK_HEAD, :] = ((t2 * cos + t1 * sin) * rinv).astype(BF16)


def _q_proj(h, w_in, glat, wqt, gq, layer, cos_t, sin_t, tm, tq, q_scale):
    b, s, d = h.shape
    per_q = tq // tm
    of_layer = lambda i, r: (layer, 0, 0)
    per_layer = lambda a: _resident((None,) + a.shape[1:], of_layer)
    n_gate = w_in.shape[2] // Q_LORA - 1
    wb = n_gate * Q_LORA
    w_in_block = lambda c: _resident((None, d, Q_LORA), lambda i, r: (layer, 0, c))
    return pl.pallas_call(
        functools.partial(_q_kernel, tm=tm, q_scale=q_scale, n_gate=n_gate),
        grid=(b, s // tm),
        in_specs=[pl.BlockSpec((1, tm, d), lambda i, r: (i, r, 0)),
                  w_in_block(0), per_layer(glat), *[w_in_block(1 + c) for c in range(n_gate)],
                  per_layer(wqt), per_layer(gq),
                  pl.BlockSpec((1, QK_ROPE // 2, tm), lambda i, r: (i, 0, r)),
                  pl.BlockSpec((1, QK_ROPE // 2, tm), lambda i, r: (i, 0, r))],
        out_specs=[pl.BlockSpec((1, N_HEADS, 1, QK_HEAD, tm), lambda i, r: (i, 0, r // per_q, 0, r % per_q)),
                   pl.BlockSpec((1, wb // V_HEAD, tm, V_HEAD), lambda i, r: (i, 0, r, 0))],
        out_shape=[jax.ShapeDtypeStruct((b, N_HEADS, s // tq, QK_HEAD, tq), BF16),
                   jax.ShapeDtypeStruct((b, wb // V_HEAD, s, V_HEAD), BF16)],
        compiler_params=_cparams(2),
        name="q_proj",
    )(h, w_in, glat, *([w_in] * n_gate), wqt, gq, cos_t, sin_t)


def _attn_kernel(qt_ref, k_ref, vt_ref, sg_ref, o_ref, *scratch, tq, tk, running_max):
    n_q = qt_ref.shape[2]
    n_chain = tq // ATT_SUB
    n_per = tq // tk
    sub_per = tk // ATT_SUB
    if running_max:
        s0_ref, s1_ref, cm0_ref, cm1_ref, m_ref, acc_ref = scratch
        cm_refs = (cm0_ref, cm1_ref)
    else:
        s0_ref, s1_ref, ps0_ref, ps1_ref, l_ref, acc_ref = scratch
        ps_refs = (ps0_ref, ps1_ref)
    s_refs = (s0_ref, s1_ref)

    def chain_cols(j):
        return slice(j * ATT_SUB, (j + 1) * ATT_SUB)

    items = []
    for qi in range(n_q):
        items += [(qi, kt, None) for kt in range(qi * n_per)]
        items += [(qi, qi * n_per + d, d) for d in range(n_per)]

    def visible_rows(item, j):
        _, _, d = item
        if d is None:
            return tk
        return min(max(j - d * sub_per + 1, 0), sub_per) * ATT_SUB

    def issue_scores(t, j):
        qi, kt, d = items[t]
        rows = visible_rows(items[t], j)
        if rows == 0:
            return
        k_tile = k_ref[0, 0, kt * tk:kt * tk + rows, :]
        st = jnp.dot(k_tile, qt_ref[0, 0, qi, :, chain_cols(j)], preferred_element_type=F32)
        if d is not None and j < (d + 1) * sub_per:
            k_chunk = lax.broadcasted_iota(jnp.int32, (rows, ATT_SUB), 0) // CHUNK
            q_chunk = lax.broadcasted_iota(jnp.int32, (rows, ATT_SUB), 1) // CHUNK
            st = jnp.where(k_chunk <= q_chunk + (rows // ATT_SUB - 1) * (ATT_SUB // CHUNK), st, -1e30)
        if running_max:
            s_refs[t % 2][j, 0:rows, :] = st
            cm_refs[t % 2][j] = jnp.max(st, axis=0, keepdims=True)
        else:
            p = jnp.exp2(st)
            s_refs[t % 2][j, 0:rows, :] = p.astype(BF16)
            ps_refs[t % 2][j] = jnp.sum(p, axis=0, keepdims=True)

    def consume(t, j):
        qi, kt, _ = items[t]
        rows = visible_rows(items[t], j)
        if rows == 0:
            return
        cols = chain_cols(j)
        vt_tile = vt_ref[0, 0, kt, :, 0:rows]
        if running_max:
            m_old = m_ref[qi, :, cols]
            m_new = jnp.maximum(m_old, cm_refs[t % 2][j])
            m_ref[qi, :, cols] = m_new
            p = jnp.exp2(s_refs[t % 2][j, 0:rows, :] - m_new).astype(BF16)
            pv = jnp.dot(vt_tile, p, preferred_element_type=F32)
            acc_ref[qi, :, cols] = jnp.exp2(m_old - m_new) * acc_ref[qi, :, cols] + pv
        else:
            l_ref[qi, :, cols] += ps_refs[t % 2][j]
            acc_ref[qi, :, cols] += jnp.dot(vt_tile[0:V_HEAD], s_refs[t % 2][j, 0:rows, :],
                                            preferred_element_type=F32)

    for j in range(n_chain):
        issue_scores(0, j)
    for t, (qi, kt, d) in enumerate(items):
        if t == 0 or items[t - 1][0] != qi:
            if running_max:
                m_ref[qi] = jnp.full(m_ref.shape[1:], -1e30, F32)
            else:
                l_ref[qi] = jnp.zeros(l_ref.shape[1:], F32)
            acc_ref[qi] = jnp.zeros(acc_ref.shape[1:], F32)
        for j in range(n_chain):
            if t + 1 < len(items):
                issue_scores(t + 1, j)
            consume(t, j)
        if t + 1 == len(items) or items[t + 1][0] != qi:
            inv_l = 1.0 / (acc_ref[qi, V_HEAD:V_HEAD + 1, :] if running_max else l_ref[qi])
            o = (acc_ref[qi, 0:V_HEAD, :] * inv_l).T
            rows_q = slice(qi * tq, (qi + 1) * tq)
            o_ref[0, 0, rows_q, :] = (o * sg_ref[0, 0, rows_q, :].astype(F32)).astype(BF16)


def _attention(qt, k, vt, sg, *, tk, running_max):
    b, n_heads, n_q, _, tq = qt.shape
    s = n_q * tq
    n_kt = s // tk
    n_chain = tq // ATT_SUB
    assert tq % tk == 0 and tk % ATT_SUB == 0
    if running_max:
        scratch_shapes = [pltpu.VMEM((n_chain, tk, ATT_SUB), F32),
                          pltpu.VMEM((n_chain, tk, ATT_SUB), F32),
                          pltpu.VMEM((n_chain, 1, ATT_SUB), F32),
                          pltpu.VMEM((n_chain, 1, ATT_SUB), F32),
                          pltpu.VMEM((n_q, 1, tq), F32),
                          pltpu.VMEM((n_q, V_ROWS, tq), F32)]
    else:
        scratch_shapes = [pltpu.VMEM((n_chain, tk, ATT_SUB), BF16),
                          pltpu.VMEM((n_chain, tk, ATT_SUB), BF16),
                          pltpu.VMEM((n_chain, 1, ATT_SUB), F32),
                          pltpu.VMEM((n_chain, 1, ATT_SUB), F32),
                          pltpu.VMEM((n_q, 1, tq), F32),
                          pltpu.VMEM((n_q, V_HEAD, tq), F32)]
    return pl.pallas_call(
        functools.partial(_attn_kernel, tq=tq, tk=tk, running_max=running_max),
        grid=(b, n_heads),
        in_specs=[pl.BlockSpec((1, 1, n_q, QK_HEAD, tq), lambda i, hd: (i, hd, 0, 0, 0)),
                  pl.BlockSpec((1, 1, s, QK_HEAD), lambda i, hd: (i, hd, 0, 0)),
                  pl.BlockSpec((1, 1, n_kt, V_ROWS, tk), lambda i, hd: (i, hd, 0, 0, 0)),
                  pl.BlockSpec((1, 1, s, V_HEAD), lambda i, hd: (i, hd, 0, 0))],
        out_specs=pl.BlockSpec((1, 1, s, V_HEAD), lambda i, hd: (i, hd, 0, 0)),
        out_shape=jax.ShapeDtypeStruct((b, n_heads, s, V_HEAD), BF16),
        scratch_shapes=scratch_shapes,
        compiler_params=_cparams(2),
        name="attn" if running_max else "attn_bounded",
    )(qt, k, vt, sg)


def _rope_tables(positions):
    inv = ROPE_THETA ** (-jnp.arange(0, QK_ROPE, 2, dtype=F32) / QK_ROPE)
    ang = inv[None, :, None] * positions.astype(F32)[:, None, :]
    return lax.optimization_barrier((jnp.cos(ang), jnp.sin(ang)))


def kernel(x, positions, a_norm_g, a_w_in, a_w_group, a_scale, a_w_out, kv_norm_g, kv_w_a, kv_latent_g, kv_w_b, k_norm_g, b_norm_g, b_w_in, b_q_latent_g, b_w_q_b, b_q_norm_g, b_w_out):
    b, s, d = x.shape
    n_a = a_w_in.shape[0]
    n_b = b_w_in.shape[0]
    n_groups, ga = a_w_group.shape[1], a_w_group.shape[2]
    tm = min(512, s)
    tm_pool = min(1024, s)
    tq = min(1024, s)
    tk = min(512, s)

    cos_t, sin_t = _rope_tables(positions)
    cos = jnp.swapaxes(cos_t, 1, 2)
    sin = jnp.swapaxes(sin_t, 1, 2)
    zeros = jnp.zeros((b, s, LANES - QK_ROPE), F32)
    cos_k = jnp.concatenate([cos, cos, zeros], axis=-1)
    sin_k = jnp.concatenate([-sin, sin, zeros], axis=-1)

    a_w_out_bf = a_w_out.astype(BF16).reshape(n_a, n_groups, ga, d)
    a_scale3 = a_scale.reshape(n_a, 1, n_groups * ga)

    h = None
    for layer in range(n_a):
        w_fold = _fold_group_weights(a_w_in, a_w_group, layer, min(1024, d))
        first = 0
        ys = []
        if layer == 0:
            y0, h = _pool_group(x, w_fold, a_w_in, a_scale3, layer, 0, tm, norm_gain=a_norm_g[0:1])
            ys, first = [y0], 1
        ys += [_pool_group(h, w_fold, a_w_in, a_scale3, layer, g, tm_pool) for g in range(first, n_groups)]
        if layer + 1 < n_a:
            x, h = _out_proj(ys, a_w_out_bf, layer, x, a_norm_g[layer + 1:layer + 2], tm)
        else:
            x, h, h_kv = _out_proj(ys, a_w_out_bf, layer, x, jnp.stack([b_norm_g[0], kv_norm_g]), tm)

    wa = jnp.pad(kv_w_a, ((0, 0), (0, LANES - QK_ROPE))).astype(BF16)
    kv_w = kv_w_b.reshape(KV_LORA, N_HEADS, QK_NOPE + V_HEAD)
    wk = kv_w[:, :, :QK_NOPE].reshape(KV_LORA, N_HEADS * QK_NOPE).astype(BF16)
    wvt = kv_w[:, :, QK_NOPE:].reshape(KV_LORA, N_HEADS * V_HEAD).T.astype(BF16)
    gkn = k_norm_g[:QK_NOPE].reshape(1, QK_NOPE)
    gkr = jnp.pad(k_norm_g[QK_NOPE:], (0, LANES - QK_ROPE)).reshape(1, LANES)
    k, vt = _shared_kv(h_kv, wa, kv_latent_g.reshape(1, KV_LORA), wk, wvt, gkn, gkr,
                       cos_k, sin_k, tm, tk)

    q_scale = (QK_HEAD ** -0.5) * math.log2(math.e)
    b_w_in_bf = b_w_in.astype(BF16)
    b_w_qt = jnp.swapaxes(b_w_q_b, 1, 2).astype(BF16)
    b_w_out_bf = b_w_out.astype(BF16)[:, None]
    b_glat = b_q_latent_g.reshape(n_b, 1, Q_LORA)
    b_gq = b_q_norm_g.reshape(n_b, QK_HEAD, 1)
    for j in range(n_b):
        qt, sg = _q_proj(h, b_w_in_bf, b_glat, b_w_qt, b_gq, j, cos_t, sin_t, tm, tq, q_scale)
        score_bound = 1.01 * QK_HEAD * q_scale * jnp.max(jnp.abs(b_q_norm_g[j])) * jnp.max(jnp.abs(k_norm_g))
        og = lax.cond(score_bound <= ATT_BOUNDED_SCORE_LIMIT,
                      functools.partial(_attention, tk=tk, running_max=False),
                      functools.partial(_attention, tk=tk, running_max=True),
                      qt, k, vt, sg)
        if j + 1 < n_b:
            x, h = _out_proj([og], b_w_out_bf, j, x, b_norm_g[j + 1:j + 2], tm)
        else:
            (x,) = _out_proj([og], b_w_out_bf, j, x, None, tm)
    return x
```

```python
import functools
import math

import jax
import jax.numpy as jnp
from jax import lax
from jax.experimental import pallas as pl
from jax.experimental.pallas import tpu as pltpu

CHUNK = 64
POOL_WINDOWS = (2, 4, 8, 16)
N_HEADS = 16
QK_NOPE = 128
QK_ROPE = 64
QK_HEAD = QK_NOPE + QK_ROPE
V_HEAD = 128
KV_LORA = 512
Q_LORA = 512
ROPE_THETA = 10000.0
EPS = 1e-6

LANES = 128
POOL_HALO = 32
ATT_SUB = 256
V_ONES = 16
V_ROWS = V_HEAD + V_ONES
ATT_BOUNDED_SCORE_LIMIT = 64.0
VMEM_LIMIT_BYTES = 56 * 1024 * 1024

F32 = jnp.float32
BF16 = jnp.bfloat16
NT_DIMS = (((1,), (1,)), ((), ()))


def _cparams(n_axes):
    return pltpu.CompilerParams(
        dimension_semantics=("arbitrary",) * n_axes,
        vmem_limit_bytes=VMEM_LIMIT_BYTES)


def _resident(shape, index_map):
    return pl.BlockSpec(shape, index_map, pipeline_mode=pl.Buffered(1))


def _silu(v):
    return v * jax.nn.sigmoid(v)


def _fold_kernel(wu_ref, wgrp_ref, o_ref):
    o_ref[...] = jnp.dot(wu_ref[...].astype(BF16), wgrp_ref[...].astype(BF16),
                         preferred_element_type=F32).astype(BF16)


def _fold_group_weights(w_in, w_group, layer, tr):
    d = w_in.shape[1]
    _, n_groups, ga, _ = w_group.shape
    return pl.pallas_call(
        _fold_kernel,
        grid=(n_groups, d // tr),
        in_specs=[pl.BlockSpec((None, tr, ga), lambda g, r: (layer, r, g)),
                  pl.BlockSpec((None, None, ga, ga), lambda g, r: (layer, g, 0, 0))],
        out_specs=pl.BlockSpec((tr, ga), lambda g, r: (r, g)),
        out_shape=jax.ShapeDtypeStruct((d, n_groups * ga), BF16),
        compiler_params=_cparams(2),
        name="fold_group_weights",
    )(w_in, w_group)


def _pool_kernel(*refs, tm, levels, normalize):
    if normalize:
        h_ref, g_ref, wu_ref, wg_ref, sc_ref, y_ref, hn_ref, a_ref, b_ref, c_ref, wgb_ref = refs
    else:
        h_ref, wu_ref, wg_ref, sc_ref, y_ref, a_ref, b_ref, c_ref, wgb_ref = refs
    r = pl.program_id(1)
    ga = a_ref.shape[1]

    @pl.when((r == 0) & (pl.program_id(0) == 0))
    def _():
        wgb_ref[...] = wg_ref[...].astype(BF16)

    @pl.when(r == 0)
    def _():
        a_ref[0:POOL_HALO, :] = jnp.zeros((POOL_HALO, ga), F32)

    if normalize:
        x = h_ref[0]
        ms = jnp.mean(x * x, axis=-1, keepdims=True)
        hn_ref[0] = (x * lax.rsqrt(ms + EPS) * g_ref[...]).astype(BF16)
        h = hn_ref[0]
    else:
        h = h_ref[0]
    a_ref[POOL_HALO:POOL_HALO + tm, :] = jnp.dot(h, wu_ref[...], preferred_element_type=F32)

    src = a_ref
    for j in range(levels):
        shift = 1 << j
        lo = 8 * (j + 1)
        n = POOL_HALO + tm - lo
        dst = (b_ref, c_ref)[j % 2]
        dst[lo:lo + n, :] = src[lo:lo + n, :] + src[lo - shift:lo - shift + n, :]
        src = dst

    window = 1 << levels
    t = r * tm + lax.broadcasted_iota(jnp.int32, (tm, LANES), 0)
    inv_count = 1.0 / jnp.minimum(t + 1, window).astype(F32)
    gate = jnp.dot(h, wgb_ref[...], preferred_element_type=F32)
    for c in range(ga // LANES):
        cols = slice(c * LANES, (c + 1) * LANES)
        z = src[POOL_HALO:POOL_HALO + tm, cols] * inv_count - a_ref[POOL_HALO:POOL_HALO + tm, cols]
        y_ref[0, :, cols] = (z * sc_ref[:, cols] * _silu(gate[:, cols])).astype(BF16)

    a_ref[0:POOL_HALO, :] = a_ref[tm:tm + POOL_HALO, :]


def _pool_group(h, w_fold, w_in, scale, layer, g, tm, norm_gain=None):
    b, s, d = h.shape
    normalize = norm_gain is not None
    n_groups = len(POOL_WINDOWS)
    ga = w_fold.shape[1] // n_groups
    levels = int(math.log2(POOL_WINDOWS[g]))
    assert 1 << levels == POOL_WINDOWS[g] and 8 * levels <= POOL_HALO <= tm
    gain_spec = [_resident((1, d), lambda i, r: (0, 0))] if normalize else []
    gain_arg = [norm_gain] if normalize else []
    out_specs = [pl.BlockSpec((1, tm, ga), lambda i, r: (i, r, 0))]
    out_shape = [jax.ShapeDtypeStruct((b, s, ga), BF16)]
    if normalize:
        out_specs.append(pl.BlockSpec((1, tm, d), lambda i, r: (i, r, 0)))
        out_shape.append(jax.ShapeDtypeStruct((b, s, d), BF16))
    outs = pl.pallas_call(
        functools.partial(_pool_kernel, tm=tm, levels=levels, normalize=normalize),
        grid=(b, s // tm),
        in_specs=[pl.BlockSpec((1, tm, d), lambda i, r: (i, r, 0)),
                  *gain_spec,
                  _resident((d, ga), lambda i, r: (0, g)),
                  _resident((None, d, ga), lambda i, r: (layer, 0, n_groups + g)),
                  _resident((None, 1, ga), lambda i, r: (layer, 0, g))],
        out_specs=out_specs,
        out_shape=out_shape,
        scratch_shapes=[pltpu.VMEM((POOL_HALO + tm, ga), F32),
                        pltpu.VMEM((POOL_HALO + tm, ga), F32),
                        pltpu.VMEM((POOL_HALO + tm, ga), F32),
                        pltpu.VMEM((d, ga), BF16)],
        compiler_params=_cparams(2),
        name=f"pool_g{g}_norm" if normalize else f"pool_g{g}",
    )(h, *gain_arg, w_fold, w_in, scale)
    return tuple(outs) if normalize else outs[0]


def _out_kernel(*refs, n_parts, n_norm):
    y_refs = refs[:n_parts]
    w_ref, x_ref = refs[n_parts], refs[n_parts + 1]
    g_ref = refs[n_parts + 2] if n_norm else None
    outs = refs[n_parts + 2 + (1 if n_norm else 0):]
    xo_ref, h_refs = outs[0], outs[1:]

    acc = x_ref[0]
    for p in range(n_parts):
        if len(y_refs[p].shape) == 4:
            y = jnp.concatenate([y_refs[p][0, hd] for hd in range(y_refs[p].shape[1])], axis=1)
        else:
            y = y_refs[p][0]
        acc = acc + jnp.dot(y, w_ref[p], preferred_element_type=F32)
    xo_ref[0] = acc
    if n_norm:
        ms = jnp.mean(acc * acc, axis=-1, keepdims=True)
        rn = acc * lax.rsqrt(ms + EPS)
        for j in range(n_norm):
            h_refs[j][0] = (rn * g_ref[j:j + 1, :]).astype(BF16)


def _out_proj(parts, w, layer, x, gains, tm):
    b, s, d = x.shape
    _, n_parts, kp, _ = w.shape
    n_norm = 0 if gains is None else gains.shape[0]
    row = lambda i, r: (i, r, 0)
    in_specs = [pl.BlockSpec((1, tm, kp), row) if part.ndim == 3 else
                pl.BlockSpec((1, part.shape[1], tm, part.shape[3]), lambda i, r: (i, 0, r, 0))
                for part in parts]
    in_specs += [_resident((None, n_parts, kp, d), lambda i, r: (layer, 0, 0, 0)),
                 pl.BlockSpec((1, tm, d), row)]
    args = list(parts) + [w, x]
    if n_norm:
        in_specs.append(_resident((n_norm, d), lambda i, r: (0, 0)))
        args.append(gains)
    out_specs = [pl.BlockSpec((1, tm, d), row) for _ in range(1 + n_norm)]
    out_shape = [jax.ShapeDtypeStruct((b, s, d), F32)]
    out_shape += [jax.ShapeDtypeStruct((b, s, d), BF16) for _ in range(n_norm)]
    return pl.pallas_call(
        functools.partial(_out_kernel, n_parts=n_parts, n_norm=n_norm),
        grid=(b, s // tm),
        in_specs=in_specs,
        out_specs=out_specs,
        out_shape=out_shape,
        compiler_params=_cparams(2),
        name=f"out_proj_p{n_parts}_n{n_norm}",
    )(*args)


def _kv_kernel(h_ref, wa_ref, glat_ref, wk_ref, wvt_ref, gkn_ref, gkr_ref, cos_ref, sin_ref,
               k_ref, vt_ref, *, tm, tkv):
    h = h_ref[0]
    ckv = jnp.dot(h, wa_ref[...], preferred_element_type=F32)
    c = ckv[:, :KV_LORA]
    kr = ckv[:, KV_LORA:]
    cn = (c * lax.rsqrt(jnp.mean(c * c, axis=-1, keepdims=True) + EPS) * glat_ref[...]).astype(BF16)
    kn = jnp.dot(cn, wk_ref[...], preferred_element_type=F32)
    vt = lax.dot_general(wvt_ref[...], cn, NT_DIMS, preferred_element_type=F32)

    ss_rope = jnp.sum(kr * kr, axis=-1, keepdims=True)
    krg = kr * gkr_ref[...]
    half = QK_ROPE // 2
    swapped = pltpu.roll(krg, half, 1) + pltpu.roll(krg, LANES - half, 1)
    cos = cos_ref[0].T
    sin = sin_ref[0].T
    pad = jnp.zeros((tm, LANES - QK_ROPE), F32)
    cos_k = jnp.concatenate([cos, cos, pad], axis=1)
    sin_k = jnp.concatenate([-sin, sin, pad], axis=1)
    roped = krg * cos_k + swapped * sin_k

    for hd in range(N_HEADS):
        knh = kn[:, hd * QK_NOPE:(hd + 1) * QK_NOPE]
        ss = jnp.sum(knh * knh, axis=-1, keepdims=True) + ss_rope
        rinv = lax.rsqrt(ss * (1.0 / QK_HEAD) + EPS)
        k_ref[0, hd, :, 0:QK_NOPE] = (knh * rinv * gkn_ref[...]).astype(BF16)
        k_ref[0, hd, :, QK_NOPE:QK_HEAD] = (roped[:, :QK_ROPE] * rinv).astype(BF16)
        for j in range(tm // tkv):
            vt_ref[0, hd, j, 0:V_HEAD, :] = vt[hd * V_HEAD:(hd + 1) * V_HEAD, j * tkv:(j + 1) * tkv].astype(BF16)
            vt_ref[0, hd, j, V_HEAD:V_ROWS, :] = jnp.ones((V_ONES, tkv), BF16)


def _shared_kv(h, wa, glat, wk, wvt, gkn, gkr, cos_t, sin_t, tm, tkv):
    b, s, d = h.shape
    row = lambda i, r: (i, r, 0)
    const2 = lambda i, r: (0, 0)
    return pl.pallas_call(
        functools.partial(_kv_kernel, tm=tm, tkv=tkv),
        grid=(b, s // tm),
        in_specs=[pl.BlockSpec((1, tm, d), row),
                  _resident(wa.shape, const2),
                  _resident(glat.shape, const2),
                  _resident(wk.shape, const2),
                  _resident(wvt.shape, const2),
                  _resident(gkn.shape, const2),
                  _resident(gkr.shape, const2),
                  pl.BlockSpec((1, QK_ROPE // 2, tm), lambda i, r: (i, 0, r)),
                  pl.BlockSpec((1, QK_ROPE // 2, tm), lambda i, r: (i, 0, r))],
        out_specs=[pl.BlockSpec((1, N_HEADS, tm, QK_HEAD), lambda i, r: (i, 0, r, 0)),
                   pl.BlockSpec((1, N_HEADS, tm // tkv, V_ROWS, tkv), lambda i, r: (i, 0, r, 0, 0))],
        out_shape=[jax.ShapeDtypeStruct((b, N_HEADS, s, QK_HEAD), BF16),
                   jax.ShapeDtypeStruct((b, N_HEADS, s // tkv, V_ROWS, tkv), BF16)],
        compiler_params=_cparams(2),
        name="shared_kv",
    )(h, wa, glat, wk, wvt, gkn, gkr, cos_t, sin_t)


def _q_kernel(*refs, tm, q_scale, n_gate):
    h_ref, wlat_ref, glat_ref = refs[:3]
    wgate_refs = refs[3:3 + n_gate]
    wqt_ref, gq_ref, cos_ref, sin_ref, qt_ref, sg_ref = refs[3 + n_gate:]
    h = h_ref[0]
    ql = jnp.dot(h, wlat_ref[...], preferred_element_type=F32)
    qn = (ql * lax.rsqrt(jnp.mean(ql * ql, axis=-1, keepdims=True) + EPS) * glat_ref[...]).astype(BF16)
    for c, wgate_ref in enumerate(wgate_refs):
        gate = jnp.dot(h, wgate_ref[...], preferred_element_type=F32)
        sg = _silu(gate).astype(BF16)
        for i in range(Q_LORA // V_HEAD):
            sg_ref[0, c * (Q_LORA // V_HEAD) + i] = sg[:, i * V_HEAD:(i + 1) * V_HEAD]

    qt = lax.dot_general(wqt_ref[...], qn, NT_DIMS, preferred_element_type=F32)
    gq = jnp.broadcast_to(gq_ref[...], (QK_HEAD, tm))
    cos = cos_ref[0]
    sin = sin_ref[0]
    half = QK_ROPE // 2
    for hd in range(N_HEADS):
        q = qt[hd * QK_HEAD:(hd + 1) * QK_HEAD, :]
        ss = jnp.sum(q * q, axis=0, keepdims=True)
        rinv = lax.rsqrt(ss * (1.0 / QK_HEAD) + EPS) * q_scale
        qg = q * gq
        t1 = qg[QK_NOPE:QK_NOPE + half]
        t2 = qg[QK_NOPE + half:]
        qt_ref[0, hd, 0, 0:QK_NOPE, :] = (qg[:QK_NOPE] * rinv).astype(BF16)
        qt_ref[0, hd, 0, QK_NOPE:QK_NOPE + half, :] = ((t1 * cos - t2 * sin) * rinv).astype(BF16)
        qt_ref[0, hd, 0, QK_NOPE + half:QK_HEAD, :] = ((t2 * cos + t1 * sin) * rinv).astype(BF16)


def _q_proj(h, w_in, glat, wqt, gq, layer, cos_t, sin_t, tm, tq, q_scale):
    b, s, d = h.shape
    per_q = tq // tm
    of_layer = lambda i, r: (layer, 0, 0)
    per_layer = lambda a: _resident((None,) + a.shape[1:], of_layer)
    n_gate = w_in.shape[2] // Q_LORA - 1
    wb = n_gate * Q_LORA
    w_in_block = lambda c: _resident((None, d, Q_LORA), lambda i, r: (layer, 0, c))
    return pl.pallas_call(
        functools.partial(_q_kernel, tm=tm, q_scale=q_scale, n_gate=n_gate),
        grid=(b, s // tm),
        in_specs=[pl.BlockSpec((1, tm, d), lambda i, r: (i, r, 0)),
                  w_in_block(0), per_layer(glat), *[w_in_block(1 + c) for c in range(n_gate)],
                  per_layer(wqt), per_layer(gq),
                  pl.BlockSpec((1, QK_ROPE // 2, tm), lambda i, r: (i, 0, r)),
                  pl.BlockSpec((1, QK_ROPE // 2, tm), lambda i, r: (i, 0, r))],
        out_specs=[pl.BlockSpec((1, N_HEADS, 1, QK_HEAD, tm), lambda i, r: (i, 0, r // per_q, 0, r % per_q)),
                   pl.BlockSpec((1, wb // V_HEAD, tm, V_HEAD), lambda i, r: (i, 0, r, 0))],
        out_shape=[jax.ShapeDtypeStruct((b, N_HEADS, s // tq, QK_HEAD, tq), BF16),
                   jax.ShapeDtypeStruct((b, wb // V_HEAD, s, V_HEAD), BF16)],
        compiler_params=_cparams(2),
        name="q_proj",
    )(h, w_in, glat, *([w_in] * n_gate), wqt, gq, cos_t, sin_t)


def _attn_kernel(qt_ref, k_ref, vt_ref, sg_ref, o_ref, *scratch, tq, tk, running_max):
    n_q = qt_ref.shape[2]
    n_chain = tq // ATT_SUB
    n_per = tq // tk
    sub_per = tk // ATT_SUB
    if running_max:
        s0_ref, s1_ref, cm0_ref, cm1_ref, m_ref, acc_ref = scratch
        cm_refs = (cm0_ref, cm1_ref)
    else:
        s0_ref, s1_ref, ps0_ref, ps1_ref, l_ref, acc_ref = scratch
        ps_refs = (ps0_ref, ps1_ref)
    s_refs = (s0_ref, s1_ref)

    def chain_cols(j):
        return slice(j * ATT_SUB, (j + 1) * ATT_SUB)

    items = []
    for qi in range(n_q):
        items += [(qi, kt, None) for kt in range(qi * n_per)]
        items += [(qi, qi * n_per + d, d) for d in range(n_per)]

    def visible_rows(item, j):
        _, _, d = item
        if d is None:
            return tk
        return min(max(j - d * sub_per + 1, 0), sub_per) * ATT_SUB

    def issue_scores(t, j):
        qi, kt, d = items[t]
        rows = visible_rows(items[t], j)
        if rows == 0:
            return
        k_tile = k_ref[0, 0, kt * tk:kt * tk + rows, :]
        st = jnp.dot(k_tile, qt_ref[0, 0, qi, :, chain_cols(j)], preferred_element_type=F32)
        if d is not None and j < (d + 1) * sub_per:
            k_chunk = lax.broadcasted_iota(jnp.int32, (rows, ATT_SUB), 0) // CHUNK
            q_chunk = lax.broadcasted_iota(jnp.int32, (rows, ATT_SUB), 1) // CHUNK
            st = jnp.where(k_chunk <= q_chunk + (rows // ATT_SUB - 1) * (ATT_SUB // CHUNK), st, -1e30)
        if running_max:
            s_refs[t % 2][j, 0:rows, :] = st
            cm_refs[t % 2][j] = jnp.max(st, axis=0, keepdims=True)
        else:
            p = jnp.exp2(st)
            s_refs[t % 2][j, 0:rows, :] = p.astype(BF16)
            ps_refs[t % 2][j] = jnp.sum(p, axis=0, keepdims=True)

    def consume(t, j):
        qi, kt, _ = items[t]
        rows = visible_rows(items[t], j)
        if rows == 0:
            return
        cols = chain_cols(j)
        vt_tile = vt_ref[0, 0, kt, :, 0:rows]
        if running_max:
            m_old = m_ref[qi, :, cols]
            m_new = jnp.maximum(m_old, cm_refs[t % 2][j])
            m_ref[qi, :, cols] = m_new
            p = jnp.exp2(s_refs[t % 2][j, 0:rows, :] - m_new).astype(BF16)
            pv = jnp.dot(vt_tile, p, preferred_element_type=F32)
            acc_ref[qi, :, cols] = jnp.exp2(m_old - m_new) * acc_ref[qi, :, cols] + pv
        else:
            l_ref[qi, :, cols] += ps_refs[t % 2][j]
            acc_ref[qi, :, cols] += jnp.dot(vt_tile[0:V_HEAD], s_refs[t % 2][j, 0:rows, :],
                                            preferred_element_type=F32)

    for j in range(n_chain):
        issue_scores(0, j)
    for t, (qi, kt, d) in enumerate(items):
        if t == 0 or items[t - 1][0] != qi:
            if running_max:
                m_ref[qi] = jnp.full(m_ref.shape[1:], -1e30, F32)
            else:
                l_ref[qi] = jnp.zeros(l_ref.shape[1:], F32)
            acc_ref[qi] = jnp.zeros(acc_ref.shape[1:], F32)
        for j in range(n_chain):
            if t + 1 < len(items):
                issue_scores(t + 1, j)
            consume(t, j)
        if t + 1 == len(items) or items[t + 1][0] != qi:
            inv_l = 1.0 / (acc_ref[qi, V_HEAD:V_HEAD + 1, :] if running_max else l_ref[qi])
            o = (acc_ref[qi, 0:V_HEAD, :] * inv_l).T
            rows_q = slice(qi * tq, (qi + 1) * tq)
            o_ref[0, 0, rows_q, :] = (o * sg_ref[0, 0, rows_q, :].astype(F32)).astype(BF16)


def _attention(qt, k, vt, sg, *, tk, running_max):
    b, n_heads, n_q, _, tq = qt.shape
    s = n_q * tq
    n_kt = s // tk
    n_chain = tq // ATT_SUB
    assert tq % tk == 0 and tk % ATT_SUB == 0
    if running_max:
        scratch_shapes = [pltpu.VMEM((n_chain, tk, ATT_SUB), F32),
                          pltpu.VMEM((n_chain, tk, ATT_SUB), F32),
                          pltpu.VMEM((n_chain, 1, ATT_SUB), F32),
                          pltpu.VMEM((n_chain, 1, ATT_SUB), F32),
                          pltpu.VMEM((n_q, 1, tq), F32),
                          pltpu.VMEM((n_q, V_ROWS, tq), F32)]
    else:
        scratch_shapes = [pltpu.VMEM((n_chain, tk, ATT_SUB), BF16),
                          pltpu.VMEM((n_chain, tk, ATT_SUB), BF16),
                          pltpu.VMEM((n_chain, 1, ATT_SUB), F32),
                          pltpu.VMEM((n_chain, 1, ATT_SUB), F32),
                          pltpu.VMEM((n_q, 1, tq), F32),
                          pltpu.VMEM((n_q, V_HEAD, tq), F32)]
    return pl.pallas_call(
        functools.partial(_attn_kernel, tq=tq, tk=tk, running_max=running_max),
        grid=(b, n_heads),
        in_specs=[pl.BlockSpec((1, 1, n_q, QK_HEAD, tq), lambda i, hd: (i, hd, 0, 0, 0)),
                  pl.BlockSpec((1, 1, s, QK_HEAD), lambda i, hd: (i, hd, 0, 0)),
                  pl.BlockSpec((1, 1, n_kt, V_ROWS, tk), lambda i, hd: (i, hd, 0, 0, 0)),
                  pl.BlockSpec((1, 1, s, V_HEAD), lambda i, hd: (i, hd, 0, 0))],
        out_specs=pl.BlockSpec((1, 1, s, V_HEAD), lambda i, hd: (i, hd, 0, 0)),
        out_shape=jax.ShapeDtypeStruct((b, n_heads, s, V_HEAD), BF16),
        scratch_shapes=scratch_shapes,
        compiler_params=_cparams(2),
        name="attn" if running_max else "attn_bounded",
    )(qt, k, vt, sg)


def _rope_tables(positions):
    inv = ROPE_THETA ** (-jnp.arange(0, QK_ROPE, 2, dtype=F32) / QK_ROPE)
    ang = inv[None, :, None] * positions.astype(F32)[:, None, :]
    return jnp.cos(ang), jnp.sin(ang)


def kernel(x, positions, a_norm_g, a_w_in, a_w_group, a_scale, a_w_out, kv_norm_g, kv_w_a, kv_latent_g, kv_w_b, k_norm_g, b_norm_g, b_w_in, b_q_latent_g, b_w_q_b, b_q_norm_g, b_w_out):
    b, s, d = x.shape
    n_a = a_w_in.shape[0]
    n_b = b_w_in.shape[0]
    n_groups, ga = a_w_group.shape[1], a_w_group.shape[2]
    tm = min(512, s)
    tm_pool = min(1024, s)
    tq = min(1024, s)
    tk = min(512, s)

    cos_t, sin_t = _rope_tables(positions)

    a_w_out_bf = a_w_out.astype(BF16).reshape(n_a, n_groups, ga, d)
    a_scale3 = a_scale.reshape(n_a, 1, n_groups * ga)

    h = None
    for layer in range(n_a):
        w_fold = _fold_group_weights(a_w_in, a_w_group, layer, min(1024, d))
        first = 0
        ys = []
        if layer == 0:
            y0, h = _pool_group(x, w_fold, a_w_in, a_scale3, layer, 0, tm, norm_gain=a_norm_g[0:1])
            ys, first = [y0], 1
        ys += [_pool_group(h, w_fold, a_w_in, a_scale3, layer, g, tm_pool) for g in range(first, n_groups)]
        if layer + 1 < n_a:
            x, h = _out_proj(ys, a_w_out_bf, layer, x, a_norm_g[layer + 1:layer + 2], tm)
        else:
            x, h, h_kv = _out_proj(ys, a_w_out_bf, layer, x, jnp.stack([b_norm_g[0], kv_norm_g]), tm)

    wa = jnp.pad(kv_w_a, ((0, 0), (0, LANES - QK_ROPE))).astype(BF16)
    kv_w = kv_w_b.reshape(KV_LORA, N_HEADS, QK_NOPE + V_HEAD)
    wk = kv_w[:, :, :QK_NOPE].reshape(KV_LORA, N_HEADS * QK_NOPE).astype(BF16)
    wvt = kv_w[:, :, QK_NOPE:].reshape(KV_LORA, N_HEADS * V_HEAD).T.astype(BF16)
    gkn = k_norm_g[:QK_NOPE].reshape(1, QK_NOPE)
    gkr = jnp.pad(k_norm_g[QK_NOPE:], (0, LANES - QK_ROPE)).reshape(1, LANES)
    k, vt = _shared_kv(h_kv, wa, kv_latent_g.reshape(1, KV_LORA), wk, wvt, gkn, gkr,
                       cos_t, sin_t, tm, tk)

    q_scale = (QK_HEAD ** -0.5) * math.log2(math.e)
    b_w_in_bf = b_w_in.astype(BF16)
    b_w_qt = jnp.swapaxes(b_w_q_b, 1, 2).astype(BF16)
    b_w_out_bf = b_w_out.astype(BF16)[:, None]
    b_glat = b_q_latent_g.reshape(n_b, 1, Q_LORA)
    b_gq = b_q_norm_g.reshape(n_b, QK_HEAD, 1)
    for j in range(n_b):
        qt, sg = _q_proj(h, b_w_in_bf, b_glat, b_w_qt, b_gq, j, cos_t, sin_t, tm, tq, q_scale)
        score_bound = 1.01 * QK_HEAD * q_scale * jnp.max(jnp.abs(b_q_norm_g[j])) * jnp.max(jnp.abs(k_norm_g))
        og = lax.cond(score_bound <= ATT_BOUNDED_SCORE_LIMIT,
                      functools.partial(_attention, tk=tk, running_max=False),
                      functools.partial(_attention, tk=tk, running_max=True),
                      qt, k, vt, sg)
        if j + 1 < n_b:
            x, h = _out_proj([og], b_w_out_bf, j, x, b_norm_g[j + 1:j + 2], tm)
        else:
            (x,) = _out_proj([og], b_w_out_bf, j, x, None, tm)
    return x
```

```python
import functools
import math

import jax
import jax.numpy as jnp
from jax import lax
from jax.experimental import pallas as pl
from jax.experimental.pallas import tpu as pltpu

CHUNK = 64
POOL_WINDOWS = (2, 4, 8, 16)
N_HEADS = 16
QK_NOPE = 128
QK_ROPE = 64
QK_HEAD = QK_NOPE + QK_ROPE
V_HEAD = 128
KV_LORA = 512
Q_LORA = 512
ROPE_THETA = 10000.0
EPS = 1e-6

LANES = 128
POOL_HALO = 32
ATT_SUB = 256
V_ONES = 16
V_ROWS = V_HEAD + V_ONES
ATT_BOUNDED_SCORE_LIMIT = 64.0
VMEM_LIMIT_BYTES = 56 * 1024 * 1024

F32 = jnp.float32
BF16 = jnp.bfloat16
NT_DIMS = (((1,), (1,)), ((), ()))


def _cparams(n_axes):
    return pltpu.CompilerParams(
        dimension_semantics=("arbitrary",) * n_axes,
        vmem_limit_bytes=VMEM_LIMIT_BYTES)


def _resident(shape, index_map):
    return pl.BlockSpec(shape, index_map, pipeline_mode=pl.Buffered(1))


def _silu(v):
    return v * jax.nn.sigmoid(v)


def _fold_kernel(wu_ref, wgrp_ref, o_ref):
    o_ref[...] = jnp.dot(wu_ref[...].astype(BF16), wgrp_ref[...].astype(BF16),
                         preferred_element_type=F32).astype(BF16)


def _fold_group_weights(w_in, w_group, layer, tr):
    d = w_in.shape[1]
    _, n_groups, ga, _ = w_group.shape
    return pl.pallas_call(
        _fold_kernel,
        grid=(n_groups, d // tr),
        in_specs=[pl.BlockSpec((None, tr, ga), lambda g, r: (layer, r, g)),
                  pl.BlockSpec((None, None, ga, ga), lambda g, r: (layer, g, 0, 0))],
        out_specs=pl.BlockSpec((tr, ga), lambda g, r: (r, g)),
        out_shape=jax.ShapeDtypeStruct((d, n_groups * ga), BF16),
        compiler_params=_cparams(2),
        name="fold_group_weights",
    )(w_in, w_group)


def _pool_kernel(*refs, tm, levels, normalize):
    if normalize:
        h_ref, g_ref, wu_ref, wg_ref, sc_ref, y_ref, hn_ref, a_ref, b_ref, c_ref, wgb_ref = refs
    else:
        h_ref, wu_ref, wg_ref, sc_ref, y_ref, a_ref, b_ref, c_ref, wgb_ref = refs
    r = pl.program_id(1)
    ga = a_ref.shape[1]

    @pl.when((r == 0) & (pl.program_id(0) == 0))
    def _():
        wgb_ref[...] = wg_ref[...].astype(BF16)

    @pl.when(r == 0)
    def _():
        a_ref[0:POOL_HALO, :] = jnp.zeros((POOL_HALO, ga), F32)

    if normalize:
        x = h_ref[0]
        ms = jnp.mean(x * x, axis=-1, keepdims=True)
        hn_ref[0] = (x * lax.rsqrt(ms + EPS) * g_ref[...]).astype(BF16)
        h = hn_ref[0]
    else:
        h = h_ref[0]
    a_ref[POOL_HALO:POOL_HALO + tm, :] = jnp.dot(h, wu_ref[...], preferred_element_type=F32)

    src = a_ref
    for j in range(levels):
        shift = 1 << j
        lo = 8 * (j + 1)
        n = POOL_HALO + tm - lo
        dst = (b_ref, c_ref)[j % 2]
        dst[lo:lo + n, :] = src[lo:lo + n, :] + src[lo - shift:lo - shift + n, :]
        src = dst

    window = 1 << levels
    t = r * tm + lax.broadcasted_iota(jnp.int32, (tm, LANES), 0)
    inv_count = 1.0 / jnp.minimum(t + 1, window).astype(F32)
    gate = jnp.dot(h, wgb_ref[...], preferred_element_type=F32)
    for c in range(ga // LANES):
        cols = slice(c * LANES, (c + 1) * LANES)
        z = src[POOL_HALO:POOL_HALO + tm, cols] * inv_count - a_ref[POOL_HALO:POOL_HALO + tm, cols]
        y_ref[0, :, cols] = (z * sc_ref[:, cols] * _silu(gate[:, cols])).astype(BF16)

    a_ref[0:POOL_HALO, :] = a_ref[tm:tm + POOL_HALO, :]


def _pool_group(h, w_fold, w_in, scale, layer, g, tm, norm_gain=None):
    b, s, d = h.shape
    normalize = norm_gain is not None
    n_groups = len(POOL_WINDOWS)
    ga = w_fold.shape[1] // n_groups
    levels = int(math.log2(POOL_WINDOWS[g]))
    assert 1 << levels == POOL_WINDOWS[g] and 8 * levels <= POOL_HALO <= tm
    gain_spec = [_resident((1, d), lambda i, r: (0, 0))] if normalize else []
    gain_arg = [norm_gain] if normalize else []
    out_specs = [pl.BlockSpec((1, tm, ga), lambda i, r: (i, r, 0))]
    out_shape = [jax.ShapeDtypeStruct((b, s, ga), BF16)]
    if normalize:
        out_specs.append(pl.BlockSpec((1, tm, d), lambda i, r: (i, r, 0)))
        out_shape.append(jax.ShapeDtypeStruct((b, s, d), BF16))
    outs = pl.pallas_call(
        functools.partial(_pool_kernel, tm=tm, levels=levels, normalize=normalize),
        grid=(b, s // tm),
        in_specs=[pl.BlockSpec((1, tm, d), lambda i, r: (i, r, 0)),
                  *gain_spec,
                  _resident((d, ga), lambda i, r: (0, g)),
                  _resident((None, d, ga), lambda i, r: (layer, 0, n_groups + g)),
                  _resident((None, 1, ga), lambda i, r: (layer, 0, g))],
        out_specs=out_specs,
        out_shape=out_shape,
        scratch_shapes=[pltpu.VMEM((POOL_HALO + tm, ga), F32),
                        pltpu.VMEM((POOL_HALO + tm, ga), F32),
                        pltpu.VMEM((POOL_HALO + tm, ga), F32),
                        pltpu.VMEM((d, ga), BF16)],
        compiler_params=_cparams(2),
        name=f"pool_g{g}_norm" if normalize else f"pool_g{g}",
    )(h, *gain_arg, w_fold, w_in, scale)
    return tuple(outs) if normalize else outs[0]


def _out_kernel(*refs, n_parts, n_norm):
    y_refs = refs[:n_parts]
    w_ref, x_ref = refs[n_parts], refs[n_parts + 1]
    g_ref = refs[n_parts + 2] if n_norm else None
    outs = refs[n_parts + 2 + (1 if n_norm else 0):]
    xo_ref, h_refs = outs[0], outs[1:]

    acc = x_ref[0]
    for p in range(n_parts):
        if len(y_refs[p].shape) == 4:
            y = jnp.concatenate([y_refs[p][0, hd] for hd in range(y_refs[p].shape[1])], axis=1)
        else:
            y = y_refs[p][0]
        acc = acc + jnp.dot(y, w_ref[p], preferred_element_type=F32)
    xo_ref[0] = acc
    if n_norm:
        ms = jnp.mean(acc * acc, axis=-1, keepdims=True)
        rn = acc * lax.rsqrt(ms + EPS)
        for j in range(n_norm):
            h_refs[j][0] = (rn * g_ref[j:j + 1, :]).astype(BF16)


def _out_proj(parts, w, layer, x, gains, tm):
    b, s, d = x.shape
    _, n_parts, kp, _ = w.shape
    n_norm = 0 if gains is None else gains.shape[0]
    row = lambda i, r: (i, r, 0)
    in_specs = [pl.BlockSpec((1, tm, kp), row) if part.ndim == 3 else
                pl.BlockSpec((1, part.shape[1], tm, part.shape[3]), lambda i, r: (i, 0, r, 0))
                for part in parts]
    in_specs += [_resident((None, n_parts, kp, d), lambda i, r: (layer, 0, 0, 0)),
                 pl.BlockSpec((1, tm, d), row)]
    args = list(parts) + [w, x]
    if n_norm:
        in_specs.append(_resident((n_norm, d), lambda i, r: (0, 0)))
        args.append(gains)
    out_specs = [pl.BlockSpec((1, tm, d), row) for _ in range(1 + n_norm)]
    out_shape = [jax.ShapeDtypeStruct((b, s, d), F32)]
    out_shape += [jax.ShapeDtypeStruct((b, s, d), BF16) for _ in range(n_norm)]
    return pl.pallas_call(
        functools.partial(_out_kernel, n_parts=n_parts, n_norm=n_norm),
        grid=(b, s // tm),
        in_specs=in_specs,
        out_specs=out_specs,
        out_shape=out_shape,
        compiler_params=_cparams(2),
        name=f"out_proj_p{n_parts}_n{n_norm}",
    )(*args)


def _kv_kernel(h_ref, wa_ref, glat_ref, wk_ref, wvt_ref, gkn_ref, gkr_ref, cos_ref, sin_ref,
               k_ref, vt_ref, *, tm, tkv):
    h = h_ref[0]
    ckv = jnp.dot(h, wa_ref[...], preferred_element_type=F32)
    c = ckv[:, :KV_LORA]
    kr = ckv[:, KV_LORA:]
    cn = (c * lax.rsqrt(jnp.mean(c * c, axis=-1, keepdims=True) + EPS) * glat_ref[...]).astype(BF16)
    kn = jnp.dot(cn, wk_ref[...], preferred_element_type=F32)
    vt = lax.dot_general(wvt_ref[...], cn, NT_DIMS, preferred_element_type=F32)

    ss_rope = jnp.sum(kr * kr, axis=-1, keepdims=True)
    krg = kr * gkr_ref[...]
    half = QK_ROPE // 2
    swapped = pltpu.roll(krg, half, 1) + pltpu.roll(krg, LANES - half, 1)
    cos = cos_ref[0].T
    sin = sin_ref[0].T
    pad = jnp.zeros((tm, LANES - QK_ROPE), F32)
    cos_k = jnp.concatenate([cos, cos, pad], axis=1)
    sin_k = jnp.concatenate([-sin, sin, pad], axis=1)
    roped = krg * cos_k + swapped * sin_k

    for hd in range(N_HEADS):
        knh = kn[:, hd * QK_NOPE:(hd + 1) * QK_NOPE]
        ss = jnp.sum(knh * knh, axis=-1, keepdims=True) + ss_rope
        rinv = lax.rsqrt(ss * (1.0 / QK_HEAD) + EPS)
        k_ref[0, hd, :, 0:QK_NOPE] = (knh * rinv * gkn_ref[...]).astype(BF16)
        k_ref[0, hd, :, QK_NOPE:QK_HEAD] = (roped[:, :QK_ROPE] * rinv).astype(BF16)
        for j in range(tm // tkv):
            vt_ref[0, hd, j, 0:V_HEAD, :] = vt[hd * V_HEAD:(hd + 1) * V_HEAD, j * tkv:(j + 1) * tkv].astype(BF16)
            vt_ref[0, hd, j, V_HEAD:V_ROWS, :] = jnp.ones((V_ONES, tkv), BF16)


def _shared_kv(h, wa, glat, wk, wvt, gkn, gkr, cos_t, sin_t, tm, tkv):
    b, s, d = h.shape
    row = lambda i, r: (i, r, 0)
    const2 = lambda i, r: (0, 0)
    return pl.pallas_call(
        functools.partial(_kv_kernel, tm=tm, tkv=tkv),
        grid=(b, s // tm),
        in_specs=[pl.BlockSpec((1, tm, d), row),
                  _resident(wa.shape, const2),
                  _resident(glat.shape, const2),
                  _resident(wk.shape, const2),
                  _resident(wvt.shape, const2),
                  _resident(gkn.shape, const2),
                  _resident(gkr.shape, const2),
                  pl.BlockSpec((1, QK_ROPE // 2, tm), lambda i, r: (i, 0, r)),
                  pl.BlockSpec((1, QK_ROPE // 2, tm), lambda i, r: (i, 0, r))],
        out_specs=[pl.BlockSpec((1, N_HEADS, tm, QK_HEAD), lambda i, r: (i, 0, r, 0)),
                   pl.BlockSpec((1, N_HEADS, tm // tkv, V_ROWS, tkv), lambda i, r: (i, 0, r, 0, 0))],
        out_shape=[jax.ShapeDtypeStruct((b, N_HEADS, s, QK_HEAD), BF16),
                   jax.ShapeDtypeStruct((b, N_HEADS, s // tkv, V_ROWS, tkv), BF16)],
        compiler_params=_cparams(2),
        name="shared_kv",
    )(h, wa, glat, wk, wvt, gkn, gkr, cos_t, sin_t)


def _q_kernel(*refs, tm, q_scale, n_gate):
    h_ref, wlat_ref, glat_ref = refs[:3]
    wgate_refs = refs[3:3 + n_gate]
    wqt_ref, gq_ref, cos_ref, sin_ref, qt_ref, sg_ref = refs[3 + n_gate:]
    h = h_ref[0]
    ql = jnp.dot(h, wlat_ref[...], preferred_element_type=F32)
    qn = (ql * lax.rsqrt(jnp.mean(ql * ql, axis=-1, keepdims=True) + EPS) * glat_ref[...]).astype(BF16)
    for c, wgate_ref in enumerate(wgate_refs):
        gate = jnp.dot(h, wgate_ref[...], preferred_element_type=F32)
        sg = _silu(gate).astype(BF16)
        for i in range(Q_LORA // V_HEAD):
            sg_ref[0, c * (Q_LORA // V_HEAD) + i] = sg[:, i * V_HEAD:(i + 1) * V_HEAD]

    qt = lax.dot_general(wqt_ref[...], qn, NT_DIMS, preferred_element_type=F32)
    gq = jnp.broadcast_to(gq_ref[...], (QK_HEAD, tm))
    cos = cos_ref[0]
    sin = sin_ref[0]
    half = QK_ROPE // 2
    for hd in range(N_HEADS):
        q = qt[hd * QK_HEAD:(hd + 1) * QK_HEAD, :]
        ss = jnp.sum(q * q, axis=0, keepdims=True)
        rinv = lax.rsqrt(ss * (1.0 / QK_HEAD) + EPS) * q_scale
        qg = q * gq
        t1 = qg[QK_NOPE:QK_NOPE + half]
        t2 = qg[QK_NOPE + half:]
        qt_ref[0, hd, 0, 0:QK_NOPE, :] = (qg[:QK_NOPE] * rinv).astype(BF16)
        qt_ref[0, hd, 0, QK_NOPE:QK_NOPE + half, :] = ((t1 * cos - t2 * sin) * rinv).astype(BF16)
        qt_ref[0, hd, 0, QK_NOPE + half:QK_HEAD, :] = ((t2 * cos + t1 * sin) * rinv).astype(BF16)


def _q_proj(h, w_in, glat, wqt, gq, layer, cos_t, sin_t, tm, tq, q_scale):
    b, s, d = h.shape
    per_q = tq // tm
    of_layer = lambda i, r: (layer, 0, 0)
    per_layer = lambda a: _resident((None,) + a.shape[1:], of_layer)
    n_gate = w_in.shape[2] // Q_LORA - 1
    wb = n_gate * Q_LORA
    w_in_block = lambda c: _resident((None, d, Q_LORA), lambda i, r: (layer, 0, c))
    return pl.pallas_call(
        functools.partial(_q_kernel, tm=tm, q_scale=q_scale, n_gate=n_gate),
        grid=(b, s // tm),
        in_specs=[pl.BlockSpec((1, tm, d), lambda i, r: (i, r, 0)),
                  w_in_block(0), per_layer(glat), *[w_in_block(1 + c) for c in range(n_gate)],
                  per_layer(wqt), per_layer(gq),
                  pl.BlockSpec((1, QK_ROPE // 2, tm), lambda i, r: (i, 0, r)),
                  pl.BlockSpec((1, QK_ROPE // 2, tm), lambda i, r: (i, 0, r))],
        out_specs=[pl.BlockSpec((1, N_HEADS, 1, QK_HEAD, tm), lambda i, r: (i, 0, r // per_q, 0, r % per_q)),
                   pl.BlockSpec((1, wb // V_HEAD, tm, V_HEAD), lambda i, r: (i, 0, r, 0))],
        out_shape=[jax.ShapeDtypeStruct((b, N_HEADS, s // tq, QK_HEAD, tq), BF16),
                   jax.ShapeDtypeStruct((b, wb // V_HEAD, s, V_HEAD), BF16)],
        compiler_params=_cparams(2),
        name="q_proj",
    )(h, w_in, glat, *([w_in] * n_gate), wqt, gq, cos_t, sin_t)


def _attn_kernel(qt_ref, k_ref, vt_ref, sg_ref, o_ref, *scratch, tq, tk, running_max):
    tv = tk
    tk = min(2 * tv, tq)
    n_q = qt_ref.shape[2]
    n_chain = tq // ATT_SUB
    n_per = tq // tk
    sub_per = tk // ATT_SUB
    if running_max:
        s0_ref, s1_ref, cm0_ref, cm1_ref, m_ref, acc_ref = scratch
        cm_refs = (cm0_ref, cm1_ref)
    else:
        s0_ref, s1_ref, ps0_ref, ps1_ref, l_ref, acc_ref = scratch
        ps_refs = (ps0_ref, ps1_ref)
    s_refs = (s0_ref, s1_ref)

    def chain_cols(j):
        return slice(j * ATT_SUB, (j + 1) * ATT_SUB)

    items = []
    for qi in range(n_q):
        items += [(qi, kt, None) for kt in range(qi * n_per)]
        items += [(qi, qi * n_per + d, d) for d in range(n_per)]

    def visible_rows(item, j):
        _, _, d = item
        if d is None:
            return tk
        return min(max(j - d * sub_per + 1, 0), sub_per) * ATT_SUB

    def issue_scores(t, j):
        qi, kt, d = items[t]
        rows = visible_rows(items[t], j)
        if rows == 0:
            return
        k_tile = k_ref[0, 0, kt * tk:kt * tk + rows, :]
        st = jnp.dot(k_tile, qt_ref[0, 0, qi, :, chain_cols(j)], preferred_element_type=F32)
        if d is not None and j < (d + 1) * sub_per:
            k_chunk = lax.broadcasted_iota(jnp.int32, (rows, ATT_SUB), 0) // CHUNK
            q_chunk = lax.broadcasted_iota(jnp.int32, (rows, ATT_SUB), 1) // CHUNK
            st = jnp.where(k_chunk <= q_chunk + (rows // ATT_SUB - 1) * (ATT_SUB // CHUNK), st, -1e30)
        if running_max:
            s_refs[t % 2][j, 0:rows, :] = st
            cm_refs[t % 2][j] = jnp.max(st, axis=0, keepdims=True)
        else:
            p = jnp.exp2(st)
            s_refs[t % 2][j, 0:rows, :] = p.astype(BF16)
            ps_refs[t % 2][j] = jnp.sum(p, axis=0, keepdims=True)

    def consume(t, j):
        qi, kt, _ = items[t]
        rows = visible_rows(items[t], j)
        if rows == 0:
            return
        cols = chain_cols(j)
        v_rows = V_ROWS if running_max else V_HEAD

        def value_matmul(p_of):
            pv = None
            for r0 in range(0, rows, tv):
                r1 = min(rows, r0 + tv)
                part = jnp.dot(vt_ref[0, 0, (kt * tk + r0) // tv, 0:v_rows, 0:r1 - r0], p_of(r0, r1),
                               preferred_element_type=F32)
                pv = part if pv is None else pv + part
            return pv

        if running_max:
            m_old = m_ref[qi, :, cols]
            m_new = jnp.maximum(m_old, cm_refs[t % 2][j])
            m_ref[qi, :, cols] = m_new
            pv = value_matmul(lambda r0, r1: jnp.exp2(s_refs[t % 2][j, r0:r1, :] - m_new).astype(BF16))
            acc_ref[qi, :, cols] = jnp.exp2(m_old - m_new) * acc_ref[qi, :, cols] + pv
        else:
            l_ref[qi, :, cols] += ps_refs[t % 2][j]
            acc_ref[qi, :, cols] += value_matmul(lambda r0, r1: s_refs[t % 2][j, r0:r1, :])

    for j in range(n_chain):
        issue_scores(0, j)
    for t, (qi, kt, d) in enumerate(items):
        if t == 0 or items[t - 1][0] != qi:
            if running_max:
                m_ref[qi] = jnp.full(m_ref.shape[1:], -1e30, F32)
            else:
                l_ref[qi] = jnp.zeros(l_ref.shape[1:], F32)
            acc_ref[qi] = jnp.zeros(acc_ref.shape[1:], F32)
        for j in range(n_chain):
            if t + 1 < len(items):
                issue_scores(t + 1, j)
            consume(t, j)
        if t + 1 == len(items) or items[t + 1][0] != qi:
            inv_l = 1.0 / (acc_ref[qi, V_HEAD:V_HEAD + 1, :] if running_max else l_ref[qi])
            o = (acc_ref[qi, 0:V_HEAD, :] * inv_l).T
            rows_q = slice(qi * tq, (qi + 1) * tq)
            o_ref[0, 0, rows_q, :] = (o * sg_ref[0, 0, rows_q, :].astype(F32)).astype(BF16)


def _attention(qt, k, vt, sg, *, tk, running_max):
    b, n_heads, n_q, _, tq = qt.shape
    s = n_q * tq
    n_kt = s // tk
    n_chain = tq // ATT_SUB
    assert tq % tk == 0 and tk % ATT_SUB == 0
    ta = min(2 * tk, tq)
    if running_max:
        scratch_shapes = [pltpu.VMEM((n_chain, ta, ATT_SUB), F32),
                          pltpu.VMEM((n_chain, ta, ATT_SUB), F32),
                          pltpu.VMEM((n_chain, 1, ATT_SUB), F32),
                          pltpu.VMEM((n_chain, 1, ATT_SUB), F32),
                          pltpu.VMEM((n_q, 1, tq), F32),
                          pltpu.VMEM((n_q, V_ROWS, tq), F32)]
    else:
        scratch_shapes = [pltpu.VMEM((n_chain, ta, ATT_SUB), BF16),
                          pltpu.VMEM((n_chain, ta, ATT_SUB), BF16),
                          pltpu.VMEM((n_chain, 1, ATT_SUB), F32),
                          pltpu.VMEM((n_chain, 1, ATT_SUB), F32),
                          pltpu.VMEM((n_q, 1, tq), F32),
                          pltpu.VMEM((n_q, V_HEAD, tq), F32)]
    return pl.pallas_call(
        functools.partial(_attn_kernel, tq=tq, tk=tk, running_max=running_max),
        grid=(b, n_heads),
        in_specs=[pl.BlockSpec((1, 1, n_q, QK_HEAD, tq), lambda i, hd: (i, hd, 0, 0, 0)),
                  pl.BlockSpec((1, 1, s, QK_HEAD), lambda i, hd: (i, hd, 0, 0)),
                  pl.BlockSpec((1, 1, n_kt, V_ROWS, tk), lambda i, hd: (i, hd, 0, 0, 0)),
                  pl.BlockSpec((1, 1, s, V_HEAD), lambda i, hd: (i, hd, 0, 0))],
        out_specs=pl.BlockSpec((1, 1, s, V_HEAD), lambda i, hd: (i, hd, 0, 0)),
        out_shape=jax.ShapeDtypeStruct((b, n_heads, s, V_HEAD), BF16),
        scratch_shapes=scratch_shapes,
        compiler_params=_cparams(2),
        name="attn" if running_max else "attn_bounded",
    )(qt, k, vt, sg)


def _rope_tables(positions):
    inv = ROPE_THETA ** (-jnp.arange(0, QK_ROPE, 2, dtype=F32) / QK_ROPE)
    ang = inv[None, :, None] * positions.astype(F32)[:, None, :]
    return jnp.cos(ang), jnp.sin(ang)


def kernel(x, positions, a_norm_g, a_w_in, a_w_group, a_scale, a_w_out, kv_norm_g, kv_w_a, kv_latent_g, kv_w_b, k_norm_g, b_norm_g, b_w_in, b_q_latent_g, b_w_q_b, b_q_norm_g, b_w_out):
    b, s, d = x.shape
    n_a = a_w_in.shape[0]
    n_b = b_w_in.shape[0]
    n_groups, ga = a_w_group.shape[1], a_w_group.shape[2]
    tm = min(512, s)
    tm_pool = min(1024, s)
    tq = min(1024, s)
    tk = min(512, s)

    cos_t, sin_t = _rope_tables(positions)

    a_w_out_bf = a_w_out.astype(BF16).reshape(n_a, n_groups, ga, d)
    a_scale3 = a_scale.reshape(n_a, 1, n_groups * ga)

    h = None
    for layer in range(n_a):
        w_fold = _fold_group_weights(a_w_in, a_w_group, layer, min(1024, d))
        first = 0
        ys = []
        if layer == 0:
            y0, h = _pool_group(x, w_fold, a_w_in, a_scale3, layer, 0, tm, norm_gain=a_norm_g[0:1])
            ys, first = [y0], 1
        ys += [_pool_group(h, w_fold, a_w_in, a_scale3, layer, g, tm_pool) for g in range(first, n_groups)]
        if layer + 1 < n_a:
            x, h = _out_proj(ys, a_w_out_bf, layer, x, a_norm_g[layer + 1:layer + 2], tm)
        else:
            x, h, h_kv = _out_proj(ys, a_w_out_bf, layer, x, jnp.stack([b_norm_g[0], kv_norm_g]), tm)

    wa = jnp.pad(kv_w_a, ((0, 0), (0, LANES - QK_ROPE))).astype(BF16)
    kv_w = kv_w_b.reshape(KV_LORA, N_HEADS, QK_NOPE + V_HEAD)
    wk = kv_w[:, :, :QK_NOPE].reshape(KV_LORA, N_HEADS * QK_NOPE).astype(BF16)
    wvt = kv_w[:, :, QK_NOPE:].reshape(KV_LORA, N_HEADS * V_HEAD).T.astype(BF16)
    gkn = k_norm_g[:QK_NOPE].reshape(1, QK_NOPE)
    gkr = jnp.pad(k_norm_g[QK_NOPE:], (0, LANES - QK_ROPE)).reshape(1, LANES)
    k, vt = _shared_kv(h_kv, wa, kv_latent_g.reshape(1, KV_LORA), wk, wvt, gkn, gkr,
                       cos_t, sin_t, tm, tk)

    q_scale = (QK_HEAD ** -0.5) * math.log2(math.e)
    b_w_in_bf = b_w_in.astype(BF16)
    b_w_qt = jnp.swapaxes(b_w_q_b, 1, 2).astype(BF16)
    b_w_out_bf = b_w_out.astype(BF16)[:, None]
    b_glat = b_q_latent_g.reshape(n_b, 1, Q_LORA)
    b_gq = b_q_norm_g.reshape(n_b, QK_HEAD, 1)
    for j in range(n_b):
        qt, sg = _q_proj(h, b_w_in_bf, b_glat, b_w_qt, b_gq, j, cos_t, sin_t, tm, tq, q_scale)
        score_bound = 1.01 * QK_HEAD * q_scale * jnp.max(jnp.abs(b_q_norm_g[j])) * jnp.max(jnp.abs(k_norm_g))
        og = lax.cond(score_bound <= ATT_BOUNDED_SCORE_LIMIT,
                      functools.partial(_attention, tk=tk, running_max=False),
                      functools.partial(_attention, tk=tk, running_max=True),
                      qt, k, vt, sg)
        if j + 1 < n_b:
            x, h = _out_proj([og], b_w_out_bf, j, x, b_norm_g[j + 1:j + 2], tm)
        else:
            (x,) = _out_proj([og], b_w_out_bf, j, x, None, tm)
    return x
```
